```python
import jax, jax.numpy as jnp
from jax import lax
import numpy as np

D_MODEL = 1024
BATCH = 2
SEQ = 8192
DEPTH = 4

N_MIXERS = 2
N_A = (DEPTH + 1) // 2
N_B = DEPTH // 2
EPS = 1e-6

CHUNK = 128
A_WIDTH = 2 * D_MODEL
A_GROUPS = 8
A_GDIM = A_WIDTH // A_GROUPS

B_WIDTH = 3 * D_MODEL // 2
B_HEADS = 6
B_HDIM = B_WIDTH // B_HEADS
B_CONV = 4
LRU_C = 8.0

D_FF = 3 * D_MODEL
FFN_CONV = 3

kernel_name = "hybrid_gmlp_rglru_convffn"


def rms_norm(x, g):
    xf = x.astype(jnp.float32)
    y = xf * lax.rsqrt(jnp.mean(xf * xf, axis=-1, keepdims=True) + EPS)
    return (y * g.astype(jnp.float32)).astype(x.dtype)


def causal_dwconv(x, w, b):
    k, c = w.shape
    y = lax.conv_general_dilated(
        x, w.astype(x.dtype)[:, None, :], window_strides=(1,), padding=[(k - 1, 0)],
        dimension_numbers=("NWC", "WIO", "NWC"), feature_group_count=c)
    return y + b.astype(x.dtype)


def mixer_a(h, w_in, v_gain, w_s, b_s, w_out):
    bsz, s, _ = h.shape
    z = jax.nn.gelu(h @ w_in)
    u, v = jnp.split(z, 2, axis=-1)
    v = rms_norm(v, v_gain)
    v = v.reshape(bsz, s // CHUNK, CHUNK, A_GROUPS, A_GDIM)
    ws = jnp.tril(w_s).astype(v.dtype)
    sv = jnp.einsum("gts,bcsgd->bctgd", ws, v) + b_s.T.astype(v.dtype)[:, :, None]
    y = u * sv.reshape(bsz, s, A_WIDTH)
    return y @ w_out


def _lin_rec_op(c1, c2):
    a1, b1 = c1
    a2, b2 = c2
    return a1 * a2, a2 * b1 + b2


def mixer_b(h, w_in, conv_w, conv_b, w_a, b_a, w_x, b_x, lam, w_out):
    bsz, s, _ = h.shape
    g, xb = jnp.split(h @ w_in, 2, axis=-1)
    g = jax.nn.gelu(g)
    xb = causal_dwconv(xb, conv_w, conv_b)
    xh = xb.reshape(bsz, s, B_HEADS, B_HDIM)
    r = jax.nn.sigmoid(jnp.einsum("bshd,hde->bshe", xh, w_a) + b_a).reshape(bsz, s, B_WIDTH)
    i = jax.nn.sigmoid(jnp.einsum("bshd,hde->bshe", xh, w_x) + b_x).reshape(bsz, s, B_WIDTH)
    log_a = LRU_C * r.astype(jnp.float32) * jax.nn.log_sigmoid(lam.astype(jnp.float32))
    a = jnp.exp(log_a)
    mult = jnp.sqrt(-jnp.expm1(2.0 * log_a))
    bterm = mult * (i * xb).astype(jnp.float32)
    _, hs = lax.associative_scan(_lin_rec_op, (a, bterm), axis=1)
    y = hs.astype(h.dtype) * g
    return y @ w_out


def conv_ffn(h, w_up, conv_w, conv_b, w_down):
    up = causal_dwconv(h @ w_up, conv_w, conv_b)
    gate, val = jnp.split(up, 2, axis=-1)
    return (jax.nn.gelu(gate) * val) @ w_down


def setup_inputs(seed: int = 0) -> dict:
    key = jax.random.key(seed)
    ks = jax.random.split(key, 24)
    nrm = jax.random.normal
    f32 = jnp.float32

    x = nrm(ks[0], (BATCH, SEQ, D_MODEL), f32)
    norm_mix = 1.0 + 0.05 * nrm(ks[1], (DEPTH, D_MODEL), f32)
    norm_ffn = 1.0 + 0.05 * nrm(ks[2], (DEPTH, D_MODEL), f32)
    norm_final = 1.0 + 0.05 * nrm(ks[3], (D_MODEL,), f32)

    a_w_in = nrm(ks[4], (N_A, D_MODEL, 2 * A_WIDTH), f32) * D_MODEL ** -0.5
    a_v_gain = 1.0 + 0.05 * nrm(ks[5], (N_A, A_WIDTH), f32)
    a_w_s = nrm(ks[6], (N_A, A_GROUPS, CHUNK, CHUNK), f32) * (0.5 * CHUNK ** -0.5)
    a_b_s = 1.0 + 0.1 * nrm(ks[7], (N_A, A_GROUPS, CHUNK), f32)
    a_w_out = nrm(ks[8], (N_A, A_WIDTH, D_MODEL), f32) * A_WIDTH ** -0.5

    b_w_in = nrm(ks[9], (N_B, D_MODEL, 2 * B_WIDTH), f32) * D_MODEL ** -0.5
    b_conv_w = nrm(ks[10], (N_B, B_CONV, B_WIDTH), f32) * B_CONV ** -0.5
    b_conv_b = 0.01 * nrm(ks[11], (N_B, B_WIDTH), f32)
    b_w_a = nrm(ks[12], (N_B, B_HEADS, B_HDIM, B_HDIM), f32) * B_HDIM ** -0.5
    b_b_a = 0.01 * nrm(ks[13], (N_B, B_HEADS, B_HDIM), f32)
    b_w_x = nrm(ks[14], (N_B, B_HEADS, B_HDIM, B_HDIM), f32) * B_HDIM ** -0.5
    b_b_x = 0.01 * nrm(ks[15], (N_B, B_HEADS, B_HDIM), f32)
    a0 = jax.random.uniform(ks[16], (N_B, B_WIDTH), f32, 0.9, 0.999)
    s0 = a0 ** (1.0 / LRU_C)
    b_lambda = jnp.log(s0) - jnp.log1p(-s0)
    b_w_out = nrm(ks[17], (N_B, B_WIDTH, D_MODEL), f32) * B_WIDTH ** -0.5

    f_w_up = nrm(ks[18], (DEPTH, D_MODEL, 2 * D_FF), f32) * D_MODEL ** -0.5
    f_conv_w = nrm(ks[19], (DEPTH, FFN_CONV, 2 * D_FF), f32) * FFN_CONV ** -0.5
    f_conv_b = 0.01 * nrm(ks[20], (DEPTH, 2 * D_FF), f32)
    f_w_down = nrm(ks[21], (DEPTH, D_FF, D_MODEL), f32) * D_FF ** -0.5

    return {"x": x, "norm_mix": norm_mix, "norm_ffn": norm_ffn, "norm_final": norm_final,
            "a_w_in": a_w_in, "a_v_gain": a_v_gain, "a_w_s": a_w_s, "a_b_s": a_b_s, "a_w_out": a_w_out,
            "b_w_in": b_w_in, "b_conv_w": b_conv_w, "b_conv_b": b_conv_b, "b_w_a": b_w_a, "b_b_a": b_b_a,
            "b_w_x": b_w_x, "b_b_x": b_b_x, "b_lambda": b_lambda, "b_w_out": b_w_out,
            "f_w_up": f_w_up, "f_conv_w": f_conv_w, "f_conv_b": f_conv_b, "f_w_down": f_w_down}


def reference(x, norm_mix, norm_ffn, norm_final,
              a_w_in, a_v_gain, a_w_s, a_b_s, a_w_out,
              b_w_in, b_conv_w, b_conv_b, b_w_a, b_b_a, b_w_x, b_b_x, b_lambda, b_w_out,
              f_w_up, f_conv_w, f_conv_b, f_w_down):
    for i in range(DEPTH):
        h = rms_norm(x, norm_mix[i])
        j = i // N_MIXERS
        if i % N_MIXERS == 0:
            x = x + mixer_a(h, a_w_in[j], a_v_gain[j], a_w_s[j], a_b_s[j], a_w_out[j])
        else:
            x = x + mixer_b(h, b_w_in[j], b_conv_w[j], b_conv_b[j], b_w_a[j], b_b_a[j],
                            b_w_x[j], b_b_x[j], b_lambda[j], b_w_out[j])
        h = rms_norm(x, norm_ffn[i])
        x = x + conv_ffn(h, f_w_up[i], f_conv_w[i], f_conv_b[i], f_w_down[i])
    return rms_norm(x, norm_final)
```

```python
import functools

import jax
import jax.numpy as jnp
from jax import lax
from jax.experimental import pallas as pl
from jax.experimental.pallas import tpu as pltpu

EPS = 1e-6
LRU_C = 8.0
CHUNK = 128
SUBLANES = 8
COL = 256
VMEM_LIMIT_BYTES = 56 * 1024 * 1024


def _rms(x, g):
    ms = jnp.mean(x * x, axis=-1, keepdims=True)
    return x * lax.rsqrt(ms + EPS) * g


def _dot(a, b):
    return jnp.dot(a, b, preferred_element_type=jnp.float32)


def _causal_taps(u, prev, w, b):
    k = w.shape[0]
    t = u.shape[0]
    ext = jnp.concatenate([prev, u], axis=0)
    y = u * w[k - 1:k, :] + b
    for d in range(1, k):
        shifted = pltpu.roll(ext, d, axis=0)[SUBLANES:SUBLANES + t, :]
        y = y + shifted * w[k - 1 - d:k - d, :]
    return y


def _const_spec(shape):
    n = len(shape)
    return pl.BlockSpec(shape, lambda b, s: (0,) * n, pipeline_mode=pl.Buffered(1))


def _row_spec(ts, d):
    return pl.BlockSpec((None, ts, d), lambda b, s: (b, s, 0))


def _params():
    return pltpu.CompilerParams(
        dimension_semantics=("arbitrary", "arbitrary"),
        vmem_limit_bytes=VMEM_LIMIT_BYTES)


def _ffn_body(x_ref, g_ref, wup_ref, cw_ref, cb_ref, wdn_ref, gf_ref, o_ref, tail_ref,
              *, d_ff, final_norm):
    @pl.when(pl.program_id(1) == 0)
    def _():
        tail_ref[...] = jnp.zeros_like(tail_ref)

    x = x_ref[...]
    ts = x.shape[0]
    h = _rms(x, g_ref[...]).astype(jnp.bfloat16)
    acc = x
    nch = d_ff // COL

    def up(j):
        return (_dot(h, wup_ref[:, j * COL:(j + 1) * COL]),
                _dot(h, wup_ref[:, d_ff + j * COL:d_ff + (j + 1) * COL]))

    nxt = up(0)
    for j in range(nch):
        cg = slice(j * COL, (j + 1) * COL)
        cv = slice(d_ff + j * COL, d_ff + (j + 1) * COL)
        ug, uv = nxt
        if j + 1 < nch:
            nxt = up(j + 1)
        gate = _causal_taps(ug, tail_ref[:, cg], cw_ref[:, cg], cb_ref[:, cg])
        val = _causal_taps(uv, tail_ref[:, cv], cw_ref[:, cv], cb_ref[:, cv])
        tail_ref[:, cg] = ug[ts - SUBLANES:, :]
        tail_ref[:, cv] = uv[ts - SUBLANES:, :]
        act = (jax.nn.gelu(gate) * val).astype(jnp.bfloat16)
        acc = acc + _dot(act, wdn_ref[cg, :])
    if final_norm:
        acc = _rms(acc, gf_ref[...])
    o_ref[...] = acc


def _ffn_call(x, g, w_up, conv_w, conv_b, w_down, g_final, *, ts, final_norm):
    bsz, s, d = x.shape
    d_ff = w_down.shape[0]
    body = functools.partial(_ffn_body, d_ff=d_ff, final_norm=final_norm)
    return pl.pallas_call(
        body,
        grid=(bsz, s // ts),
        in_specs=[_row_spec(ts, d), _const_spec((1, d)), _const_spec(w_up.shape),
                  _const_spec(conv_w.shape), _const_spec((1, 2 * d_ff)),
                  _const_spec(w_down.shape), _const_spec((1, d))],
        out_specs=_row_spec(ts, d),
        out_shape=jax.ShapeDtypeStruct(x.shape, x.dtype),
        scratch_shapes=[pltpu.VMEM((SUBLANES, 2 * d_ff), jnp.float32)],
        compiler_params=_params(),
        name="conv_ffn",
    )(x, g.reshape(1, d), w_up, conv_w, conv_b.reshape(1, -1), w_down, g_final.reshape(1, d))


def _mixer_a_body(x_ref, g_ref, win_ref, vg_ref, ws_ref, bs_ref, wout_ref, o_ref,
                  *, width, groups):
    x = x_ref[...]
    ts = x.shape[0]
    gdim = width // groups
    h = _rms(x, g_ref[...]).astype(jnp.bfloat16)

    us, vs = [], []
    ss = jnp.zeros((ts, 1), jnp.float32)
    for c in range(groups):
        us.append(jax.nn.gelu(_dot(h, win_ref[:, c * gdim:(c + 1) * gdim])))
    for c in range(groups):
        v = jax.nn.gelu(_dot(h, win_ref[:, width + c * gdim:width + (c + 1) * gdim]))
        ss = ss + jnp.sum(v * v, axis=-1, keepdims=True)
        vs.append(v)
    scale = lax.rsqrt(ss * (1.0 / width) + EPS)

    row = lax.broadcasted_iota(jnp.int32, (CHUNK, CHUNK), 0)
    col = lax.broadcasted_iota(jnp.int32, (CHUNK, CHUNK), 1)
    acc = x
    for c in range(groups):
        cs = slice(c * gdim, (c + 1) * gdim)
        vn = (vs[c] * scale * vg_ref[:, cs]).astype(jnp.bfloat16)
        wt = jnp.where(row >= col, ws_ref[c], 0.0).astype(jnp.bfloat16)
        bias = bs_ref[c]
        sv = jnp.concatenate(
            [_dot(wt, vn[k * CHUNK:(k + 1) * CHUNK, :]) + bias for k in range(ts // CHUNK)],
            axis=0)
        y = (us[c] * sv).astype(jnp.bfloat16)
        acc = acc + _dot(y, wout_ref[cs, :])
    o_ref[...] = acc


def _mixer_a_call(x, g, w_in, v_gain, w_s, b_s, w_out, *, ts):
    bsz, s, d = x.shape
    width = w_out.shape[0]
    groups = w_s.shape[0]
    body = functools.partial(_mixer_a_body, width=width, groups=groups)
    return pl.pallas_call(
        body,
        grid=(bsz, s // ts),
        in_specs=[_row_spec(ts, d), _const_spec((1, d)), _const_spec(w_in.shape),
                  _const_spec((1, width)), _const_spec(w_s.shape),
                  _const_spec((groups, CHUNK, 1)), _const_spec(w_out.shape)],
        out_specs=_row_spec(ts, d),
        out_shape=jax.ShapeDtypeStruct(x.shape, x.dtype),
        compiler_params=_params(),
        name="mixer_gmlp",
    )(x, g.reshape(1, d), w_in, v_gain.reshape(1, width), w_s, b_s[:, :, None], w_out)


def _group_scan(a, b):
    t, c = a.shape
    pos = lax.broadcasted_iota(jnp.int32, (t, c), 0) % SUBLANES
    d = 1
    while d < SUBLANES:
        keep = pos >= d
        a_prev = jnp.where(keep, pltpu.roll(a, d, axis=0), 1.0)
        b_prev = jnp.where(keep, pltpu.roll(b, d, axis=0), 0.0)
        b = a * b_prev + b
        a = a * a_prev
        d *= 2
    return a, b


def _mixer_b_body(x_ref, g_ref, win_ref, cw_ref, cb_ref, wa_ref, ba_ref, wx_ref, bx_ref,
                  lam_ref, wout_ref, o_ref, tail_ref, state_ref, *, width, heads):
    @pl.when(pl.program_id(1) == 0)
    def _():
        tail_ref[...] = jnp.zeros_like(tail_ref)
        state_ref[...] = jnp.zeros_like(state_ref)

    x = x_ref[...]
    ts = x.shape[0]
    hdim = width // heads
    h = _rms(x, g_ref[...]).astype(jnp.bfloat16)
    acc = x
    for hd in range(heads):
        cs = slice(hd * hdim, (hd + 1) * hdim)
        gate = jax.nn.gelu(_dot(h, win_ref[:, cs]))
        xb_pre = _dot(h, win_ref[:, width + hd * hdim:width + (hd + 1) * hdim])
        xb = _causal_taps(xb_pre, tail_ref[:, cs], cw_ref[:, cs], cb_ref[:, cs])
        tail_ref[:, cs] = xb_pre[ts - SUBLANES:, :]
        xb16 = xb.astype(jnp.bfloat16)
        r = jax.nn.sigmoid(_dot(xb16, wa_ref[hd]) + ba_ref[:, cs])
        i = jax.nn.sigmoid(_dot(xb16, wx_ref[hd]) + bx_ref[:, cs])
        log_a = LRU_C * r * jax.nn.log_sigmoid(lam_ref[:, cs])
        a = jnp.exp(log_a)
        mult = jnp.sqrt(-jnp.tanh(log_a) * (a * a + 1.0))
        bterm = mult * (i * xb)

        a_cum, h_loc = _group_scan(a, bterm)
        carry = state_ref[:, cs]
        rows = []
        for k in range(ts // SUBLANES):
            rs = slice(k * SUBLANES, (k + 1) * SUBLANES)
            hk = h_loc[rs, :] + a_cum[rs, :] * carry
            rows.append(hk)
            carry = jnp.broadcast_to(hk[SUBLANES - 1:SUBLANES, :], (SUBLANES, hdim))
        state_ref[:, cs] = carry
        hs = jnp.concatenate(rows, axis=0)

        y = (hs * gate).astype(jnp.bfloat16)
        acc = acc + _dot(y, wout_ref[cs, :])
    o_ref[...] = acc


def _mixer_b_call(x, g, w_in, conv_w, conv_b, w_a, b_a, w_x, b_x, lam, w_out, *, ts):
    bsz, s, d = x.shape
    width = w_out.shape[0]
    heads = w_a.shape[0]
    body = functools.partial(_mixer_b_body, width=width, heads=heads)
    vec = lambda v: v.reshape(1, width)
    return pl.pallas_call(
        body,
        grid=(bsz, s // ts),
        in_specs=[_row_spec(ts, d), _const_spec((1, d)), _const_spec(w_in.shape),
                  _const_spec(conv_w.shape), _const_spec((1, width)),
                  _const_spec(w_a.shape), _const_spec((1, width)),
                  _const_spec(w_x.shape), _const_spec((1, width)),
                  _const_spec((1, width)), _const_spec(w_out.shape)],
        out_specs=_row_spec(ts, d),
        out_shape=jax.ShapeDtypeStruct(x.shape, x.dtype),
        scratch_shapes=[pltpu.VMEM((SUBLANES, width), jnp.float32),
                        pltpu.VMEM((SUBLANES, width), jnp.float32)],
        compiler_params=_params(),
        name="mixer_rglru",
    )(x, g.reshape(1, d), w_in, conv_w, vec(conv_b), w_a, vec(b_a), w_x, vec(b_x),
      vec(lam), w_out)


def kernel(x, norm_mix, norm_ffn, norm_final, a_w_in, a_v_gain, a_w_s, a_b_s, a_w_out,
           b_w_in, b_conv_w, b_conv_b, b_w_a, b_b_a, b_w_x, b_b_x, b_lambda, b_w_out,
           f_w_up, f_conv_w, f_conv_b, f_w_down):
    bf = lambda w: w.astype(jnp.bfloat16)
    depth = norm_mix.shape[0]
    ts = 512
    for i in range(depth):
        j = i // 2
        if i % 2 == 0:
            x = _mixer_a_call(x, norm_mix[i], bf(a_w_in[j]), a_v_gain[j], a_w_s[j], a_b_s[j],
                              bf(a_w_out[j]), ts=ts)
        else:
            x = _mixer_b_call(x, norm_mix[i], bf(b_w_in[j]), b_conv_w[j], b_conv_b[j],
                              bf(b_w_a[j]), b_b_a[j], bf(b_w_x[j]), b_b_x[j], b_lambda[j],
                              bf(b_w_out[j]), ts=ts)
        x = _ffn_call(x, norm_ffn[i], bf(f_w_up[i]), f_conv_w[i], f_conv_b[i], bf(f_w_down[i]),
                      norm_final, ts=ts, final_norm=(i == depth - 1))
    return x
```

```python
import functools

import jax
import jax.numpy as jnp
from jax import lax
from jax.experimental import pallas as pl
from jax.experimental.pallas import tpu as pltpu

EPS = 1e-6
LRU_C = 8.0
SUBLANES = 8
LANES = 128
SEG = 128
TS = SUBLANES * SEG
ROWS = 32
CW = 512
VMEM_LIMIT_BYTES = 58 * 1024 * 1024


def _rms(x, g):
    ms = jnp.mean(x * x, axis=-1, keepdims=True)
    return x * lax.rsqrt(ms + EPS) * g


def _dot(a, b):
    return jnp.dot(a, b, preferred_element_type=jnp.float32)


def _const_spec(shape):
    n = len(shape)
    return pl.BlockSpec(shape, lambda b, s: (0,) * n, pipeline_mode=pl.Buffered(1))


def _row_spec(d):
    return pl.BlockSpec((None, TS, d), lambda b, s: (b, s, 0))


def _params():
    return pltpu.CompilerParams(
        dimension_semantics=("arbitrary", "arbitrary"),
        vmem_limit_bytes=VMEM_LIMIT_BYTES)


def _single_trip():
    return jnp.minimum(pl.program_id(1), 0) + 1


def _fill_head(u_ref, lanes, tail_ref, tail_cols, ktaps):
    head = SUBLANES * (ktaps - 1)
    width = lanes.stop - lanes.start
    first = lax.broadcasted_iota(jnp.int32, (SUBLANES, width), 0) == 0
    for d in range(1, ktaps):
        cur = u_ref[head + TS - SUBLANES * d:head + TS - SUBLANES * (d - 1), lanes]
        prev = tail_ref[SUBLANES * (d - 1):SUBLANES * d, tail_cols]
        u_ref[head - SUBLANES * d:head - SUBLANES * (d - 1), lanes] = jnp.where(
            first, pltpu.roll(prev, 1, axis=0), pltpu.roll(cur, 1, axis=0))
        tail_ref[SUBLANES * (d - 1):SUBLANES * d, tail_cols] = cur


def _taps(u_ref, row0, lane0, cw_ref, cb_ref, c0):
    ktaps = cw_ref.shape[0]
    y = cb_ref[:, c0:c0 + LANES]
    for k in range(ktaps):
        r0 = row0 - SUBLANES * (ktaps - 1 - k)
        if not isinstance(r0, int):
            r0 = pl.multiple_of(r0, SUBLANES)
        y = y + u_ref[pl.ds(r0, ROWS), lane0:lane0 + LANES] * cw_ref[k:k + 1, c0:c0 + LANES]
    return y


def _ffn_body(x_ref, g_ref, wup_ref, cw_ref, cb_ref, wdn_ref, gf_ref, o_ref,
              tail_ref, h_ref, u0_ref, u1_ref, act0_ref, act1_ref, *, d_ff, final_norm):
    @pl.when(pl.program_id(1) == 0)
    def _():
        tail_ref[...] = jnp.zeros_like(tail_ref)

    h_ref[...] = _rms(x_ref[...], g_ref[...]).astype(jnp.bfloat16)
    o_ref[...] = x_ref[...]
    nch = d_ff // CW
    ktaps = cw_ref.shape[0]
    head = SUBLANES * (ktaps - 1)
    u_refs = (u0_ref, u1_ref)
    act_refs = (act0_ref, act1_ref)

    def up(c):
        u_ref = u_refs[c % 2]
        for half, c0 in enumerate((c * CW, d_ff + c * CW)):
            lanes = slice(half * CW, (half + 1) * CW)
            u_ref[head:, lanes] = _dot(h_ref[...], wup_ref[:, c0:c0 + CW])
            _fill_head(u_ref, lanes, tail_ref, slice(c0, c0 + CW), ktaps)

    def gate_mul(c):
        u_ref, act_ref = u_refs[c % 2], act_refs[c % 2]
        for lc in range(CW // LANES):
            for r in range(TS // ROWS):
                row0 = head + r * ROWS
                gate = _taps(u_ref, row0, lc * LANES, cw_ref, cb_ref, c * CW + lc * LANES)
                val = _taps(u_ref, row0, CW + lc * LANES, cw_ref, cb_ref,
                            d_ff + c * CW + lc * LANES)
                act = jax.nn.gelu(gate) * val
                act_ref[r * ROWS:(r + 1) * ROWS, lc * LANES:(lc + 1) * LANES] = (
                    act.astype(jnp.bfloat16))

    def down(c):
        o_ref[...] += _dot(act_refs[c % 2][...], wdn_ref[c * CW:(c + 1) * CW, :])

    @pl.loop(0, _single_trip())
    def _(_):
        up(0)
        for c in range(nch):
            if c + 1 < nch:
                up(c + 1)
            gate_mul(c)
            down(c)

    if final_norm:
        o_ref[...] = _rms(o_ref[...], gf_ref[...])


def _ffn_call(x, g, w_up, conv_w, conv_b, w_down, g_final, *, final_norm):
    bsz, s, d = x.shape
    d_ff = w_down.shape[0]
    ktaps = conv_w.shape[0]
    head = SUBLANES * (ktaps - 1)
    body = functools.partial(_ffn_body, d_ff=d_ff, final_norm=final_norm)
    return pl.pallas_call(
        body,
        grid=(bsz, s // TS),
        in_specs=[_row_spec(d), _const_spec((1, d)), _const_spec(w_up.shape),
                  _const_spec(conv_w.shape), _const_spec((1, 2 * d_ff)),
                  _const_spec(w_down.shape), _const_spec((1, d))],
        out_specs=_row_spec(d),
        out_shape=jax.ShapeDtypeStruct(x.shape, x.dtype),
        scratch_shapes=[pltpu.VMEM((head, 2 * d_ff), jnp.float32),
                        pltpu.VMEM((TS, d), jnp.bfloat16),
                        pltpu.VMEM((head + TS, 2 * CW), jnp.float32),
                        pltpu.VMEM((head + TS, 2 * CW), jnp.float32),
                        pltpu.VMEM((TS, CW), jnp.bfloat16),
                        pltpu.VMEM((TS, CW), jnp.bfloat16)],
        compiler_params=_params(),
        name="conv_ffn",
    )(x, g.reshape(1, d), w_up, conv_w, conv_b.reshape(1, -1), w_down, g_final.reshape(1, d))


def _mixer_a_body(x_ref, g_ref, win_ref, vg_ref, ws_ref, bs_ref, wout_ref, o_ref,
                  h_ref, v_ref, sc_ref, u_ref, sv_ref, y_ref, *, width, groups):
    gdim = width // groups
    nl = gdim // LANES
    h_ref[...] = _rms(x_ref[...], g_ref[...]).astype(jnp.bfloat16)
    o_ref[...] = x_ref[...]

    ss = jnp.zeros((TS, 1), jnp.float32)
    for c in range(groups):
        v = jax.nn.gelu(_dot(h_ref[...], win_ref[:, width + c * gdim:width + (c + 1) * gdim]))
        ss = ss + jnp.sum(v * v, axis=-1, keepdims=True)
        for l in range(nl):
            v_ref[c * nl + l] = v[:, l * LANES:(l + 1) * LANES]
    sc_ref[...] = jnp.broadcast_to(lax.rsqrt(ss * (1.0 / width) + EPS), (TS, LANES))

    row = lax.broadcasted_iota(jnp.int32, (SEG, SEG), 0)
    col = lax.broadcasted_iota(jnp.int32, (SEG, SEG), 1)
    for c in range(groups):
        cs = slice(c * gdim, (c + 1) * gdim)
        u_ref[...] = jax.nn.gelu(_dot(h_ref[...], win_ref[:, cs]))
        wt = jnp.where(row >= col, ws_ref[c], 0.0).astype(jnp.bfloat16)
        bias = bs_ref[c]
        for j in range(SUBLANES):
            rows = pl.ds(j, SEG, stride=SUBLANES)
            scale = sc_ref[rows, :]
            vn = jnp.concatenate(
                [v_ref[c * nl + l, rows, :] * scale * vg_ref[:, c * gdim + l * LANES:
                                                             c * gdim + (l + 1) * LANES]
                 for l in range(nl)], axis=1).astype(jnp.bfloat16)
            sv = _dot(wt, vn) + bias
            for l in range(nl):
                sv_ref[l, rows, :] = sv[:, l * LANES:(l + 1) * LANES]
        for l in range(nl):
            y_ref[:, l * LANES:(l + 1) * LANES] = (
                u_ref[:, l * LANES:(l + 1) * LANES] * sv_ref[l]).astype(jnp.bfloat16)
        o_ref[...] += _dot(y_ref[...], wout_ref[cs, :])


def _mixer_a_call(x, g, w_in, v_gain, w_s, b_s, w_out):
    bsz, s, d = x.shape
    width = w_out.shape[0]
    groups = w_s.shape[0]
    gdim = width // groups
    assert w_s.shape[1] == SEG
    body = functools.partial(_mixer_a_body, width=width, groups=groups)
    return pl.pallas_call(
        body,
        grid=(bsz, s // TS),
        in_specs=[_row_spec(d), _const_spec((1, d)), _const_spec(w_in.shape),
                  _const_spec((1, width)), _const_spec(w_s.shape),
                  _const_spec((groups, SEG, 1)), _const_spec(w_out.shape)],
        out_specs=_row_spec(d),
        out_shape=jax.ShapeDtypeStruct(x.shape, x.dtype),
        scratch_shapes=[pltpu.VMEM((TS, d), jnp.bfloat16),
                        pltpu.VMEM((width // LANES, TS, LANES), jnp.float32),
                        pltpu.VMEM((TS, LANES), jnp.float32),
                        pltpu.VMEM((TS, gdim), jnp.float32),
                        pltpu.VMEM((gdim // LANES, TS, LANES), jnp.float32),
                        pltpu.VMEM((TS, gdim), jnp.bfloat16)],
        compiler_params=_params(),
        name="mixer_gmlp",
    )(x, g.reshape(1, d), w_in, v_gain.reshape(1, width), w_s, b_s[:, :, None], w_out)


def _mixer_b_body(x_ref, g_ref, win_ref, cw_ref, cb_ref, wa_ref, ba_ref, wx_ref, bx_ref,
                  lam_ref, wout_ref, o_ref, tail_ref, state_ref, h_ref, xp_ref, gate_ref,
                  xb_ref, xb16_ref, r_ref, i_ref, a_ref, b_ref, y_ref, *, width, heads):
    @pl.when(pl.program_id(1) == 0)
    def _():
        tail_ref[...] = jnp.zeros_like(tail_ref)
        state_ref[...] = jnp.zeros_like(state_ref)

    hdim = width // heads
    ktaps = cw_ref.shape[0]
    head = SUBLANES * (ktaps - 1)
    h_ref[...] = _rms(x_ref[...], g_ref[...]).astype(jnp.bfloat16)
    o_ref[...] = x_ref[...]
    sub = lax.broadcasted_iota(jnp.int32, (SUBLANES, CW), 0)

    for c in range(width // CW):
        c0 = c * CW
        gate_ref[...] = jax.nn.gelu(_dot(h_ref[...], win_ref[:, c0:c0 + CW]))
        xp_ref[head:, :] = _dot(h_ref[...], win_ref[:, width + c0:width + c0 + CW])
        _fill_head(xp_ref, slice(0, CW), tail_ref, slice(c0, c0 + CW), ktaps)

        @pl.loop(0, TS // ROWS)
        def _(r):
            row0 = pl.multiple_of(r * ROWS, ROWS)
            for lc in range(CW // LANES):
                xb = _taps(xp_ref, head + row0, lc * LANES, cw_ref, cb_ref, c0 + lc * LANES)
                xb_ref[pl.ds(row0, ROWS), lc * LANES:(lc + 1) * LANES] = xb
                xb16_ref[pl.ds(row0, ROWS), lc * LANES:(lc + 1) * LANES] = xb.astype(jnp.bfloat16)

        for k in range(CW // hdim):
            hd = c * (CW // hdim) + k
            hs = slice(k * hdim, (k + 1) * hdim)
            r_ref[:, hs] = _dot(xb16_ref[:, hs], wa_ref[hd])
            i_ref[:, hs] = _dot(xb16_ref[:, hs], wx_ref[hd])

        log_s = LRU_C * jax.nn.log_sigmoid(lam_ref[:, c0:c0 + CW])

        @pl.loop(0, TS // ROWS)
        def _(r):
            rows = pl.ds(pl.multiple_of(r * ROWS, ROWS), ROWS)
            for lc in range(CW // LANES):
                ls = slice(lc * LANES, (lc + 1) * LANES)
                cs = slice(c0 + lc * LANES, c0 + (lc + 1) * LANES)
                rg = jax.nn.sigmoid(r_ref[rows, ls] + ba_ref[:, cs])
                ig = jax.nn.sigmoid(i_ref[rows, ls] + bx_ref[:, cs])
                log_a = rg * log_s[:, ls]
                a = jnp.exp(log_a)
                mult = jnp.sqrt(-jnp.tanh(log_a) * (a * a + 1.0))
                a_ref[rows, ls] = a
                b_ref[rows, ls] = mult * (ig * xb_ref[rows, ls])

        def scan_step(t, carry):
            hprev, pprev = carry
            rows = pl.ds(pl.multiple_of(t * SUBLANES, SUBLANES), SUBLANES)
            a = a_ref[rows, :]
            hcur = a * hprev + b_ref[rows, :]
            pcur = a * pprev
            b_ref[rows, :] = hcur
            a_ref[rows, :] = pcur
            return hcur, pcur

        h_end, p_end = lax.fori_loop(
            0, SEG, scan_step,
            (jnp.zeros((SUBLANES, CW), jnp.float32), jnp.ones((SUBLANES, CW), jnp.float32)),
            unroll=8)

        inflow = jnp.where(sub == 0, state_ref[:, c0:c0 + CW], 0.0)
        for j in range(1, SUBLANES):
            nxt = pltpu.roll(h_end + p_end * inflow, 1, axis=0)
            inflow = jnp.where(sub == j, nxt, inflow)
        last = (h_end + p_end * inflow)[SUBLANES - 1:SUBLANES, :]
        state_ref[:, c0:c0 + CW] = jnp.broadcast_to(last, (SUBLANES, CW))

        @pl.loop(0, TS // ROWS)
        def _(r):
            rows = pl.ds(pl.multiple_of(r * ROWS, ROWS), ROWS)
            inflow_rows = jnp.concatenate([inflow] * (ROWS // SUBLANES), axis=0)
            hs_full = b_ref[rows, :] + a_ref[rows, :] * inflow_rows
            y_ref[rows, :] = (hs_full * gate_ref[rows, :]).astype(jnp.bfloat16)

        o_ref[...] += _dot(y_ref[...], wout_ref[c0:c0 + CW, :])


def _mixer_b_call(x, g, w_in, conv_w, conv_b, w_a, b_a, w_x, b_x, lam, w_out):
    bsz, s, d = x.shape
    width = w_out.shape[0]
    heads = w_a.shape[0]
    ktaps = conv_w.shape[0]
    head = SUBLANES * (ktaps - 1)
    body = functools.partial(_mixer_b_body, width=width, heads=heads)
    vec = lambda v: v.reshape(1, width)
    f32 = jnp.float32
    return pl.pallas_call(
        body,
        grid=(bsz, s // TS),
        in_specs=[_row_spec(d), _const_spec((1, d)), _const_spec(w_in.shape),
                  _const_spec(conv_w.shape), _const_spec((1, width)),
                  _const_spec(w_a.shape), _const_spec((1, width)),
                  _const_spec(w_x.shape), _const_spec((1, width)),
                  _const_spec((1, width)), _const_spec(w_out.shape)],
        out_specs=_row_spec(d),
        out_shape=jax.ShapeDtypeStruct(x.shape, x.dtype),
        scratch_shapes=[pltpu.VMEM((head, width), f32),
                        pltpu.VMEM((SUBLANES, width), f32),
                        pltpu.VMEM((TS, d), jnp.bfloat16),
                        pltpu.VMEM((head + TS, CW), f32),
                        pltpu.VMEM((TS, CW), f32),
                        pltpu.VMEM((TS, CW), f32),
                        pltpu.VMEM((TS, CW), jnp.bfloat16),
                        pltpu.VMEM((TS, CW), f32),
                        pltpu.VMEM((TS, CW), f32),
                        pltpu.VMEM((TS, CW), f32),
                        pltpu.VMEM((TS, CW), f32),
                        pltpu.VMEM((TS, CW), jnp.bfloat16)],
        compiler_params=_params(),
        name="mixer_rglru",
    )(x, g.reshape(1, d), w_in, conv_w, vec(conv_b), w_a, vec(b_a), w_x, vec(b_x),
      vec(lam), w_out)


def _interleave_segments(x):
    bsz, s, d = x.shape
    y = x.reshape(bsz, s // TS, SUBLANES, SEG, d)
    return jnp.swapaxes(y, 2, 3).reshape(bsz, s, d)


def _deinterleave_segments(x):
    bsz, s, d = x.shape
    y = x.reshape(bsz, s // TS, SEG, SUBLANES, d)
    return jnp.swapaxes(y, 2, 3).reshape(bsz, s, d)


def kernel(x, norm_mix, norm_ffn, norm_final, a_w_in, a_v_gain, a_w_s, a_b_s, a_w_out,
           b_w_in, b_conv_w, b_conv_b, b_w_a, b_b_a, b_w_x, b_b_x, b_lambda, b_w_out,
           f_w_up, f_conv_w, f_conv_b, f_w_down):
    bf = lambda w: w.astype(jnp.bfloat16)
    depth = norm_mix.shape[0]
    x = _interleave_segments(x)
    for i in range(depth):
        j = i // 2
        if i % 2 == 0:
            x = _mixer_a_call(x, norm_mix[i], bf(a_w_in[j]), a_v_gain[j], a_w_s[j], a_b_s[j],
                              bf(a_w_out[j]))
        else:
            x = _mixer_b_call(x, norm_mix[i], bf(b_w_in[j]), b_conv_w[j], b_conv_b[j],
                              bf(b_w_a[j]), b_b_a[j], bf(b_w_x[j]), b_b_x[j], b_lambda[j],
                              bf(b_w_out[j]))
        x = _ffn_call(x, norm_ffn[i], bf(f_w_up[i]), f_conv_w[i], f_conv_b[i], bf(f_w_down[i]),
                      norm_final, final_norm=(i == depth - 1))
    return _deinterleave_segments(x)
```

```python
import functools

import jax
import jax.numpy as jnp
from jax import lax
from jax.experimental import pallas as pl
from jax.experimental.pallas import tpu as pltpu

EPS = 1e-6
LRU_C = 8.0
SUBLANES = 8
LANES = 128
SEG = 128
TS = SUBLANES * SEG
ROWS = 32
CW = 512
VMEM_LIMIT_BYTES = 58 * 1024 * 1024


def _rms(x, g):
    ms = jnp.mean(x * x, axis=-1, keepdims=True)
    return x * lax.rsqrt(ms + EPS) * g


def _dot(a, b):
    return jnp.dot(a, b, preferred_element_type=jnp.float32)


def _sigmoid(z):
    return 0.5 * jnp.tanh(0.5 * z) + 0.5


def _layer_spec(stacked, layer):
    n = stacked.ndim - 1
    return pl.BlockSpec((None,) + stacked.shape[1:], lambda b, s: (layer,) + (0,) * n,
                        pipeline_mode=pl.Buffered(1))


def _row_spec(d):
    return pl.BlockSpec((None, TS, d), lambda b, s: (b, s, 0))


def _params():
    return pltpu.CompilerParams(
        dimension_semantics=("arbitrary", "arbitrary"),
        vmem_limit_bytes=VMEM_LIMIT_BYTES)


def _single_trip():
    return jnp.minimum(pl.program_id(1), 0) + 1


def _fill_head(u_ref, lanes, tail_ref, tail_cols, ktaps):
    head = SUBLANES * (ktaps - 1)
    width = lanes.stop - lanes.start
    first = lax.broadcasted_iota(jnp.int32, (SUBLANES, width), 0) == 0
    for d in range(1, ktaps):
        cur = u_ref[head + TS - SUBLANES * d:head + TS - SUBLANES * (d - 1), lanes]
        prev = tail_ref[SUBLANES * (d - 1):SUBLANES * d, tail_cols]
        u_ref[head - SUBLANES * d:head - SUBLANES * (d - 1), lanes] = jnp.where(
            first, pltpu.roll(prev, 1, axis=0), pltpu.roll(cur, 1, axis=0))
        tail_ref[SUBLANES * (d - 1):SUBLANES * d, tail_cols] = cur


def _taps(u_ref, row0, lane0, cw_ref, cb_ref, c0):
    ktaps = cw_ref.shape[0]
    y = cb_ref[:, c0:c0 + LANES]
    for k in range(ktaps):
        r0 = row0 - SUBLANES * (ktaps - 1 - k)
        y = y + u_ref[r0:r0 + ROWS, lane0:lane0 + LANES] * cw_ref[k:k + 1, c0:c0 + LANES]
    return y


def _ffn_body(x_ref, g_ref, wup_ref, cw_ref, cb_ref, wdn_ref, gf_ref, o_ref,
              tail_ref, h_ref, u0_ref, u1_ref, act0_ref, act1_ref, *, d_ff, final_norm):
    @pl.when(pl.program_id(1) == 0)
    def _():
        tail_ref[...] = jnp.zeros_like(tail_ref)

    h_ref[...] = _rms(x_ref[...], g_ref[...]).astype(jnp.bfloat16)
    o_ref[...] = x_ref[...]
    nch = d_ff // CW
    ktaps = cw_ref.shape[0]
    head = SUBLANES * (ktaps - 1)
    u_refs = (u0_ref, u1_ref)
    act_refs = (act0_ref, act1_ref)

    def up(c):
        u_ref = u_refs[c % 2]
        for half, c0 in enumerate((c * CW, d_ff + c * CW)):
            lanes = slice(half * CW, (half + 1) * CW)
            u_ref[head:, lanes] = _dot(h_ref[...], wup_ref[:, c0:c0 + CW])
            _fill_head(u_ref, lanes, tail_ref, slice(c0, c0 + CW), ktaps)

    def gate_mul(c):
        u_ref, act_ref = u_refs[c % 2], act_refs[c % 2]
        for lc in range(CW // LANES):
            for r in range(TS // ROWS):
                row0 = head + r * ROWS
                gate = _taps(u_ref, row0, lc * LANES, cw_ref, cb_ref, c * CW + lc * LANES)
                val = _taps(u_ref, row0, CW + lc * LANES, cw_ref, cb_ref,
                            d_ff + c * CW + lc * LANES)
                act = jax.nn.gelu(gate) * val
                act_ref[r * ROWS:(r + 1) * ROWS, lc * LANES:(lc + 1) * LANES] = (
                    act.astype(jnp.bfloat16))

    def down(c):
        o_ref[...] += _dot(act_refs[c % 2][...], wdn_ref[c * CW:(c + 1) * CW, :])

    @pl.loop(0, _single_trip())
    def _(_):
        up(0)
        for c in range(nch):
            if c + 1 < nch:
                up(c + 1)
            gate_mul(c)
            down(c)

    if final_norm:
        o_ref[...] = _rms(o_ref[...], gf_ref[...])


def _ffn_call(x, layer, g, w_up, conv_w, conv_b, w_down, g_final, *, final_norm):
    bsz, s, d = x.shape
    d_ff = w_down.shape[1]
    head = SUBLANES * (conv_w.shape[1] - 1)
    g, conv_b = g[:, None, :], conv_b[:, None, :]
    g_final = g_final.reshape(1, 1, d)
    body = functools.partial(_ffn_body, d_ff=d_ff, final_norm=final_norm)
    return pl.pallas_call(
        body,
        grid=(bsz, s // TS),
        in_specs=[_row_spec(d), _layer_spec(g, layer), _layer_spec(w_up, layer),
                  _layer_spec(conv_w, layer), _layer_spec(conv_b, layer),
                  _layer_spec(w_down, layer), _layer_spec(g_final, 0)],
        out_specs=_row_spec(d),
        out_shape=jax.ShapeDtypeStruct(x.shape, x.dtype),
        scratch_shapes=[pltpu.VMEM((head, 2 * d_ff), jnp.float32),
                        pltpu.VMEM((TS, d), jnp.bfloat16),
                        pltpu.VMEM((head + TS, 2 * CW), jnp.float32),
                        pltpu.VMEM((head + TS, 2 * CW), jnp.float32),
                        pltpu.VMEM((TS, CW), jnp.bfloat16),
                        pltpu.VMEM((TS, CW), jnp.bfloat16)],
        compiler_params=_params(),
        name="conv_ffn",
    )(x, g, w_up, conv_w, conv_b, w_down, g_final)


def _mixer_a_body(x_ref, g_ref, win_ref, vg_ref, ws_ref, bs_ref, wout_ref, o_ref,
                  h_ref, v_ref, sc_ref, u_ref, sv_ref, y_ref, *, width, groups):
    gdim = width // groups
    nl = gdim // LANES
    h_ref[...] = _rms(x_ref[...], g_ref[...]).astype(jnp.bfloat16)
    o_ref[...] = x_ref[...]

    ss = jnp.zeros((TS, 1), jnp.float32)
    for c in range(groups):
        v = jax.nn.gelu(_dot(h_ref[...], win_ref[:, width + c * gdim:width + (c + 1) * gdim]))
        ss = ss + jnp.sum(v * v, axis=-1, keepdims=True)
        for l in range(nl):
            v_ref[c * nl + l] = v[:, l * LANES:(l + 1) * LANES]
    sc_ref[...] = jnp.broadcast_to(lax.rsqrt(ss * (1.0 / width) + EPS), (TS, LANES))

    row = lax.broadcasted_iota(jnp.int32, (SEG, SEG), 0)
    col = lax.broadcasted_iota(jnp.int32, (SEG, SEG), 1)
    for c in range(groups):
        cs = slice(c * gdim, (c + 1) * gdim)
        u_ref[...] = jax.nn.gelu(_dot(h_ref[...], win_ref[:, cs]))
        wt = jnp.where(row >= col, ws_ref[c], 0.0).astype(jnp.bfloat16)
        bias = bs_ref[c]
        for j in range(SUBLANES):
            rows = pl.ds(j, SEG, stride=SUBLANES)
            scale = sc_ref[rows, :]
            vn = jnp.concatenate(
                [v_ref[c * nl + l, rows, :] * scale * vg_ref[:, c * gdim + l * LANES:
                                                             c * gdim + (l + 1) * LANES]
                 for l in range(nl)], axis=1).astype(jnp.bfloat16)
            sv = _dot(wt, vn) + bias
            for l in range(nl):
                sv_ref[l, rows, :] = sv[:, l * LANES:(l + 1) * LANES]
        for l in range(nl):
            y_ref[:, l * LANES:(l + 1) * LANES] = (
                u_ref[:, l * LANES:(l + 1) * LANES] * sv_ref[l]).astype(jnp.bfloat16)
        o_ref[...] += _dot(y_ref[...], wout_ref[cs, :])


def _mixer_a_call(x, layer, g, w_in, v_gain, w_s, b_s, w_out):
    bsz, s, d = x.shape
    width = w_out.shape[1]
    groups = w_s.shape[1]
    gdim = width // groups
    assert w_s.shape[2] == SEG
    g, v_gain, b_s = g[:, None, :], v_gain[:, None, :], b_s[:, :, :, None]
    body = functools.partial(_mixer_a_body, width=width, groups=groups)
    return pl.pallas_call(
        body,
        grid=(bsz, s // TS),
        in_specs=[_row_spec(d), _layer_spec(g, 2 * layer), _layer_spec(w_in, layer),
                  _layer_spec(v_gain, layer), _layer_spec(w_s, layer),
                  _layer_spec(b_s, layer), _layer_spec(w_out, layer)],
        out_specs=_row_spec(d),
        out_shape=jax.ShapeDtypeStruct(x.shape, x.dtype),
        scratch_shapes=[pltpu.VMEM((TS, d), jnp.bfloat16),
                        pltpu.VMEM((width // LANES, TS, LANES), jnp.float32),
                        pltpu.VMEM((TS, LANES), jnp.float32),
                        pltpu.VMEM((TS, gdim), jnp.float32),
                        pltpu.VMEM((gdim // LANES, TS, LANES), jnp.float32),
                        pltpu.VMEM((TS, gdim), jnp.bfloat16)],
        compiler_params=_params(),
        name="mixer_gmlp",
    )(x, g, w_in, v_gain, w_s, b_s, w_out)


def _mixer_b_body(x_ref, g_ref, win_ref, cw_ref, cb_ref, wa_ref, ba_ref, wx_ref, bx_ref,
                  lam_ref, wout_ref, o_ref, tail_ref, state_ref, h_ref, xp0_ref, xp1_ref,
                  gate0_ref, gate1_ref, xb_ref, xb16_ref, r_ref, i_ref, a_ref, b_ref, y_ref,
                  *, width, heads):
    @pl.when(pl.program_id(1) == 0)
    def _():
        tail_ref[...] = jnp.zeros_like(tail_ref)
        state_ref[...] = jnp.zeros_like(state_ref)

    hdim = width // heads
    ktaps = cw_ref.shape[0]
    head = SUBLANES * (ktaps - 1)
    nch = width // CW
    nl = CW // LANES
    xp_refs = (xp0_ref, xp1_ref)
    gate_refs = (gate0_ref, gate1_ref)
    h_ref[...] = _rms(x_ref[...], g_ref[...]).astype(jnp.bfloat16)
    o_ref[...] = x_ref[...]
    sub = lax.broadcasted_iota(jnp.int32, (SUBLANES, CW), 0)

    def project(c):
        c0 = c * CW
        xp_ref = xp_refs[c % 2]
        gate_refs[c % 2][...] = jax.nn.gelu(_dot(h_ref[...], win_ref[:, c0:c0 + CW]))
        xp_ref[head:, :] = _dot(h_ref[...], win_ref[:, width + c0:width + c0 + CW])
        _fill_head(xp_ref, slice(0, CW), tail_ref, slice(c0, c0 + CW), ktaps)

    def recur(c):
        c0 = c * CW
        xp_ref, gate_ref = xp_refs[c % 2], gate_refs[c % 2]

        for r in range(TS // ROWS):
            rows = slice(r * ROWS, (r + 1) * ROWS)
            for l in range(nl):
                ls = slice(l * LANES, (l + 1) * LANES)
                xb = _taps(xp_ref, head + r * ROWS, l * LANES, cw_ref, cb_ref, c0 + l * LANES)
                xb_ref[rows, ls] = xb
                xb16_ref[rows, ls] = xb.astype(jnp.bfloat16)

        for k in range(CW // hdim):
            hd = c * (CW // hdim) + k
            hs = slice(k * hdim, (k + 1) * hdim)
            r_ref[:, hs] = _dot(xb16_ref[:, hs], wa_ref[hd])
            i_ref[:, hs] = _dot(xb16_ref[:, hs], wx_ref[hd])

        log_s = LRU_C * jax.nn.log_sigmoid(lam_ref[:, c0:c0 + CW])
        for r in range(TS // ROWS):
            rows = slice(r * ROWS, (r + 1) * ROWS)
            for l in range(nl):
                ls = slice(l * LANES, (l + 1) * LANES)
                cs = slice(c0 + l * LANES, c0 + (l + 1) * LANES)
                rg = _sigmoid(r_ref[rows, ls] + ba_ref[:, cs])
                ig = _sigmoid(i_ref[rows, ls] + bx_ref[:, cs])
                log_a = rg * log_s[:, ls]
                a = jnp.exp(log_a)
                mult = jnp.sqrt(-jnp.tanh(log_a) * (a * a + 1.0))
                a_ref[rows, ls] = a
                b_ref[rows, ls] = mult * (ig * xb_ref[rows, ls])

        h_run = jnp.zeros((SUBLANES, CW), jnp.float32)
        p_run = jnp.ones((SUBLANES, CW), jnp.float32)
        for t in range(SEG):
            rows = slice(t * SUBLANES, (t + 1) * SUBLANES)
            a = a_ref[rows, :]
            h_run = a * h_run + b_ref[rows, :]
            p_run = a * p_run
            b_ref[rows, :] = h_run
            a_ref[rows, :] = p_run

        inflow = jnp.where(sub == 0, state_ref[:, c0:c0 + CW], 0.0)
        for j in range(1, SUBLANES):
            nxt = pltpu.roll(h_run + p_run * inflow, 1, axis=0)
            inflow = jnp.where(sub == j, nxt, inflow)
        last = (h_run + p_run * inflow)[SUBLANES - 1:SUBLANES, :]
        state_ref[:, c0:c0 + CW] = jnp.broadcast_to(last, (SUBLANES, CW))

        inflow_rows = jnp.concatenate([inflow] * (ROWS // SUBLANES), axis=0)
        for r in range(TS // ROWS):
            rows = slice(r * ROWS, (r + 1) * ROWS)
            hs_full = b_ref[rows, :] + a_ref[rows, :] * inflow_rows
            y_ref[rows, :] = (hs_full * gate_ref[rows, :]).astype(jnp.bfloat16)

        o_ref[...] += _dot(y_ref[...], wout_ref[c0:c0 + CW, :])

    @pl.loop(0, _single_trip())
    def _(_):
        for c in range(nch):
            project(c)
            recur(c)


def _mixer_b_call(x, layer, g, w_in, conv_w, conv_b, w_a, b_a, w_x, b_x, lam, w_out):
    bsz, s, d = x.shape
    width = w_out.shape[1]
    heads = w_a.shape[1]
    head = SUBLANES * (conv_w.shape[1] - 1)
    vec = lambda v: v.reshape(v.shape[0], 1, width)
    g, conv_b, b_a, b_x, lam = g[:, None, :], vec(conv_b), vec(b_a), vec(b_x), vec(lam)
    body = functools.partial(_mixer_b_body, width=width, heads=heads)
    f32 = jnp.float32
    return pl.pallas_call(
        body,
        grid=(bsz, s // TS),
        in_specs=[_row_spec(d), _layer_spec(g, 2 * layer + 1), _layer_spec(w_in, layer),
                  _layer_spec(conv_w, layer), _layer_spec(conv_b, layer),
                  _layer_spec(w_a, layer), _layer_spec(b_a, layer),
                  _layer_spec(w_x, layer), _layer_spec(b_x, layer),
                  _layer_spec(lam, layer), _layer_spec(w_out, layer)],
        out_specs=_row_spec(d),
        out_shape=jax.ShapeDtypeStruct(x.shape, x.dtype),
        scratch_shapes=[pltpu.VMEM((head, width), f32),
                        pltpu.VMEM((SUBLANES, width), f32),
                        pltpu.VMEM((TS, d), jnp.bfloat16),
                        pltpu.VMEM((head + TS, CW), f32),
                        pltpu.VMEM((head + TS, CW), f32),
                        pltpu.VMEM((TS, CW), f32),
                        pltpu.VMEM((TS, CW), f32),
                        pltpu.VMEM((TS, CW), f32),
                        pltpu.VMEM((TS, CW), jnp.bfloat16),
                        pltpu.VMEM((TS, CW), f32),
                        pltpu.VMEM((TS, CW), f32),
                        pltpu.VMEM((TS, CW), f32),
                        pltpu.VMEM((TS, CW), f32),
                        pltpu.VMEM((TS, CW), jnp.bfloat16)],
        compiler_params=_params(),
        name="mixer_rglru",
    )(x, g, w_in, conv_w, conv_b, w_a, b_a, w_x, b_x, lam, w_out)


def _interleave_segments(x):
    bsz, s, d = x.shape
    y = x.reshape(bsz, s // TS, SUBLANES, SEG, d)
    return jnp.swapaxes(y, 2, 3).reshape(bsz, s, d)


def _deinterleave_segments(x):
    bsz, s, d = x.shape
    y = x.reshape(bsz, s // TS, SEG, SUBLANES, d)
    return jnp.swapaxes(y, 2, 3).reshape(bsz, s, d)


def kernel(x, norm_mix, norm_ffn, norm_final, a_w_in, a_v_gain, a_w_s, a_b_s, a_w_out,
           b_w_in, b_conv_w, b_conv_b, b_w_a, b_b_a, b_w_x, b_b_x, b_lambda, b_w_out,
           f_w_up, f_conv_w, f_conv_b, f_w_down):
    bf = lambda w: w.astype(jnp.bfloat16)
    a_w_in, a_w_out = bf(a_w_in), bf(a_w_out)
    b_w_in, b_w_a, b_w_x, b_w_out = bf(b_w_in), bf(b_w_a), bf(b_w_x), bf(b_w_out)
    f_w_up, f_w_down = bf(f_w_up), bf(f_w_down)
    depth = norm_mix.shape[0]
    x = _interleave_segments(x)
    for i in range(depth):
        j = i // 2
        if i % 2 == 0:
            x = _mixer_a_call(x, j, norm_mix, a_w_in, a_v_gain, a_w_s, a_b_s, a_w_out)
        else:
            x = _mixer_b_call(x, j, norm_mix, b_w_in, b_conv_w, b_conv_b, b_w_a, b_b_a,
                              b_w_x, b_b_x, b_lambda, b_w_out)
        x = _ffn_call(x, i, norm_ffn, f_w_up, f_conv_w, f_conv_b, f_w_down, norm_final,
                      final_norm=(i == depth - 1))
    return _deinterleave_segments(x)
```

```python
import functools

import jax
import jax.numpy as jnp
from jax import lax
from jax.experimental import pallas as pl
from jax.experimental.pallas import tpu as pltpu

EPS = 1e-6
LRU_C = 8.0
SUBLANES = 8
LANES = 128
SEG = 128
TS = SUBLANES * SEG
ROWS = 32
CW = 512
VMEM_LIMIT_BYTES = 58 * 1024 * 1024


def _rms(x, g):
    ms = jnp.mean(x * x, axis=-1, keepdims=True)
    return x * lax.rsqrt(ms + EPS) * g


def _dot(a, b):
    return jnp.dot(a, b, preferred_element_type=jnp.float32)


def _sigmoid(z):
    return 0.5 * jnp.tanh(0.5 * z) + 0.5


def _layer_spec(stacked, layer):
    n = stacked.ndim - 1
    return pl.BlockSpec((None,) + stacked.shape[1:], lambda b, s: (layer,) + (0,) * n,
                        pipeline_mode=pl.Buffered(1))


def _row_spec(d):
    return pl.BlockSpec((None, TS, d), lambda b, s: (b, s, 0))


def _params():
    return pltpu.CompilerParams(
        dimension_semantics=("arbitrary", "arbitrary"),
        vmem_limit_bytes=VMEM_LIMIT_BYTES)


def _single_trip():
    return jnp.minimum(pl.program_id(1), 0) + 1


def _fill_head(u_ref, lanes, tail_ref, tail_cols, ktaps):
    head = SUBLANES * (ktaps - 1)
    width = lanes.stop - lanes.start
    first = lax.broadcasted_iota(jnp.int32, (SUBLANES, width), 0) == 0
    for d in range(1, ktaps):
        cur = u_ref[head + TS - SUBLANES * d:head + TS - SUBLANES * (d - 1), lanes]
        prev = tail_ref[SUBLANES * (d - 1):SUBLANES * d, tail_cols]
        u_ref[head - SUBLANES * d:head - SUBLANES * (d - 1), lanes] = jnp.where(
            first, pltpu.roll(prev, 1, axis=0), pltpu.roll(cur, 1, axis=0))
        tail_ref[SUBLANES * (d - 1):SUBLANES * d, tail_cols] = cur


def _taps(u_ref, row0, lane0, cw_ref, cb_ref, c0):
    ktaps = cw_ref.shape[0]
    y = cb_ref[:, c0:c0 + LANES]
    for k in range(ktaps):
        r0 = row0 - SUBLANES * (ktaps - 1 - k)
        y = y + u_ref[r0:r0 + ROWS, lane0:lane0 + LANES] * cw_ref[k:k + 1, c0:c0 + LANES]
    return y


def _ffn_body(x_ref, g_ref, wup_ref, cw_ref, cb_ref, wdn_ref, gf_ref, o_ref,
              tail_ref, h_ref, u0_ref, u1_ref, act0_ref, act1_ref, perm_ref,
              *, d_ff, final_norm, natural_out):
    @pl.when(pl.program_id(1) == 0)
    def _():
        tail_ref[...] = jnp.zeros_like(tail_ref)

    h_ref[...] = _rms(x_ref[...], g_ref[...]).astype(jnp.bfloat16)
    o_ref[...] = x_ref[...]
    nch = d_ff // CW
    ktaps = cw_ref.shape[0]
    head = SUBLANES * (ktaps - 1)
    u_refs = (u0_ref, u1_ref)
    act_refs = (act0_ref, act1_ref)

    def up(c):
        u_ref = u_refs[c % 2]
        for half, c0 in enumerate((c * CW, d_ff + c * CW)):
            lanes = slice(half * CW, (half + 1) * CW)
            u_ref[head:, lanes] = _dot(h_ref[...], wup_ref[:, c0:c0 + CW])
            _fill_head(u_ref, lanes, tail_ref, slice(c0, c0 + CW), ktaps)

    def gate_mul(c):
        u_ref, act_ref = u_refs[c % 2], act_refs[c % 2]
        for lc in range(CW // LANES):
            for r in range(TS // ROWS):
                row0 = head + r * ROWS
                gate = _taps(u_ref, row0, lc * LANES, cw_ref, cb_ref, c * CW + lc * LANES)
                val = _taps(u_ref, row0, CW + lc * LANES, cw_ref, cb_ref,
                            d_ff + c * CW + lc * LANES)
                act = jax.nn.gelu(gate) * val
                act_ref[r * ROWS:(r + 1) * ROWS, lc * LANES:(lc + 1) * LANES] = (
                    act.astype(jnp.bfloat16))

    def down(c):
        o_ref[...] += _dot(act_refs[c % 2][...], wdn_ref[c * CW:(c + 1) * CW, :])

    @pl.loop(0, _single_trip())
    def _(_):
        up(0)
        for c in range(nch):
            if c + 1 < nch:
                up(c + 1)
            gate_mul(c)
            down(c)

    res = _rms(o_ref[...], gf_ref[...]) if final_norm else o_ref[...]
    if natural_out:
        for l in range(perm_ref.shape[0]):
            perm_ref[l] = res[:, l * LANES:(l + 1) * LANES]
        for j in range(SUBLANES):
            for l in range(perm_ref.shape[0]):
                o_ref[j * SEG:(j + 1) * SEG, l * LANES:(l + 1) * LANES] = (
                    perm_ref[l, pl.ds(j, SEG, stride=SUBLANES), :])
    elif final_norm:
        o_ref[...] = res


def _perm_scratch(d, used):
    return pltpu.VMEM((d // LANES, TS if used else SUBLANES, LANES), jnp.float32)


def _ffn_call(x, layer, g, w_up, conv_w, conv_b, w_down, g_final, *, final_norm, natural_out):
    bsz, s, d = x.shape
    d_ff = w_down.shape[1]
    head = SUBLANES * (conv_w.shape[1] - 1)
    g, conv_b = g[:, None, :], conv_b[:, None, :]
    g_final = g_final.reshape(1, 1, d)
    body = functools.partial(_ffn_body, d_ff=d_ff, final_norm=final_norm,
                             natural_out=natural_out)
    return pl.pallas_call(
        body,
        grid=(bsz, s // TS),
        in_specs=[_row_spec(d), _layer_spec(g, layer), _layer_spec(w_up, layer),
                  _layer_spec(conv_w, layer), _layer_spec(conv_b, layer),
                  _layer_spec(w_down, layer), _layer_spec(g_final, 0)],
        out_specs=_row_spec(d),
        out_shape=jax.ShapeDtypeStruct(x.shape, x.dtype),
        scratch_shapes=[pltpu.VMEM((head, 2 * d_ff), jnp.float32),
                        pltpu.VMEM((TS, d), jnp.bfloat16),
                        pltpu.VMEM((head + TS, 2 * CW), jnp.float32),
                        pltpu.VMEM((head + TS, 2 * CW), jnp.float32),
                        pltpu.VMEM((TS, CW), jnp.bfloat16),
                        pltpu.VMEM((TS, CW), jnp.bfloat16),
                        _perm_scratch(d, natural_out)],
        compiler_params=_params(),
        name="conv_ffn",
    )(x, g, w_up, conv_w, conv_b, w_down, g_final)


def _mixer_a_body(x_ref, g_ref, win_ref, vg_ref, ws_ref, bs_ref, wout_ref, o_ref,
                  h_ref, v_ref, sc_ref, u0_ref, u1_ref, sv_ref, y_ref, perm_ref,
                  *, width, groups, natural_in):
    gdim = width // groups
    nl = gdim // LANES
    u_refs = (u0_ref, u1_ref)
    if natural_in:
        for l in range(perm_ref.shape[0]):
            for j in range(SUBLANES):
                perm_ref[l, pl.ds(j, SEG, stride=SUBLANES), :] = (
                    x_ref[j * SEG:(j + 1) * SEG, l * LANES:(l + 1) * LANES])
        x = jnp.concatenate([perm_ref[l] for l in range(perm_ref.shape[0])], axis=1)
    else:
        x = x_ref[...]
    h_ref[...] = _rms(x, g_ref[...]).astype(jnp.bfloat16)
    o_ref[...] = x

    ss = jnp.zeros((TS, 1), jnp.float32)
    for c in range(groups):
        v = jax.nn.gelu(_dot(h_ref[...], win_ref[:, width + c * gdim:width + (c + 1) * gdim]))
        ss = ss + jnp.sum(v * v, axis=-1, keepdims=True)
        for l in range(nl):
            v_ref[c * nl + l] = v[:, l * LANES:(l + 1) * LANES]
    sc_ref[...] = jnp.broadcast_to(lax.rsqrt(ss * (1.0 / width) + EPS), (TS, LANES))

    row = lax.broadcasted_iota(jnp.int32, (SEG, SEG), 0)
    col = lax.broadcasted_iota(jnp.int32, (SEG, SEG), 1)
    def project_u(c):
        u_refs[c % 2][...] = jax.nn.gelu(_dot(h_ref[...], win_ref[:, c * gdim:(c + 1) * gdim]))

    project_u(0)
    for c in range(groups):
        cs = slice(c * gdim, (c + 1) * gdim)
        u_ref = u_refs[c % 2]
        if c + 1 < groups:
            project_u(c + 1)
        wt = jnp.where(row >= col, ws_ref[c], 0.0).astype(jnp.bfloat16)
        bias = bs_ref[c]
        for j in range(SUBLANES):
            rows = pl.ds(j, SEG, stride=SUBLANES)
            scale = sc_ref[rows, :]
            vn = jnp.concatenate(
                [v_ref[c * nl + l, rows, :] * scale * vg_ref[:, c * gdim + l * LANES:
                                                             c * gdim + (l + 1) * LANES]
                 for l in range(nl)], axis=1).astype(jnp.bfloat16)
            sv = _dot(wt, vn) + bias
            for l in range(nl):
                sv_ref[l, rows, :] = sv[:, l * LANES:(l + 1) * LANES]
        for l in range(nl):
            y_ref[:, l * LANES:(l + 1) * LANES] = (
                u_ref[:, l * LANES:(l + 1) * LANES] * sv_ref[l]).astype(jnp.bfloat16)
        o_ref[...] += _dot(y_ref[...], wout_ref[cs, :])


def _mixer_a_call(x, layer, g, w_in, v_gain, w_s, b_s, w_out, *, natural_in):
    bsz, s, d = x.shape
    width = w_out.shape[1]
    groups = w_s.shape[1]
    gdim = width // groups
    assert w_s.shape[2] == SEG
    g, v_gain, b_s = g[:, None, :], v_gain[:, None, :], b_s[:, :, :, None]
    body = functools.partial(_mixer_a_body, width=width, groups=groups, natural_in=natural_in)
    return pl.pallas_call(
        body,
        grid=(bsz, s // TS),
        in_specs=[_row_spec(d), _layer_spec(g, 2 * layer), _layer_spec(w_in, layer),
                  _layer_spec(v_gain, layer), _layer_spec(w_s, layer),
                  _layer_spec(b_s, layer), _layer_spec(w_out, layer)],
        out_specs=_row_spec(d),
        out_shape=jax.ShapeDtypeStruct(x.shape, x.dtype),
        scratch_shapes=[pltpu.VMEM((TS, d), jnp.bfloat16),
                        pltpu.VMEM((width // LANES, TS, LANES), jnp.float32),
                        pltpu.VMEM((TS, LANES), jnp.float32),
                        pltpu.VMEM((TS, gdim), jnp.float32),
                        pltpu.VMEM((TS, gdim), jnp.float32),
                        pltpu.VMEM((gdim // LANES, TS, LANES), jnp.float32),
                        pltpu.VMEM((TS, gdim), jnp.bfloat16),
                        _perm_scratch(d, natural_in)],
        compiler_params=_params(),
        name="mixer_gmlp",
    )(x, g, w_in, v_gain, w_s, b_s, w_out)


def _mixer_b_body(x_ref, g_ref, win_ref, cw_ref, cb_ref, wa_ref, ba_ref, wx_ref, bx_ref,
                  lam_ref, wout_ref, o_ref, tail_ref, state_ref, h_ref, xp0_ref, xp1_ref,
                  gate0_ref, gate1_ref, xb_ref, xb16_ref, r_ref, i_ref, a_ref, b_ref, y_ref,
                  *, width, heads):
    @pl.when(pl.program_id(1) == 0)
    def _():
        tail_ref[...] = jnp.zeros_like(tail_ref)
        state_ref[...] = jnp.zeros_like(state_ref)

    hdim = width // heads
    ktaps = cw_ref.shape[0]
    head = SUBLANES * (ktaps - 1)
    nch = width // CW
    nl = CW // LANES
    xp_refs = (xp0_ref, xp1_ref)
    gate_refs = (gate0_ref, gate1_ref)
    h_ref[...] = _rms(x_ref[...], g_ref[...]).astype(jnp.bfloat16)
    o_ref[...] = x_ref[...]
    sub = lax.broadcasted_iota(jnp.int32, (SUBLANES, CW), 0)

    def project(c):
        c0 = c * CW
        xp_ref = xp_refs[c % 2]
        gate_refs[c % 2][...] = jax.nn.gelu(_dot(h_ref[...], win_ref[:, c0:c0 + CW]))
        xp_ref[head:, :] = _dot(h_ref[...], win_ref[:, width + c0:width + c0 + CW])
        _fill_head(xp_ref, slice(0, CW), tail_ref, slice(c0, c0 + CW), ktaps)

    def recur(c):
        c0 = c * CW
        xp_ref, gate_ref = xp_refs[c % 2], gate_refs[c % 2]

        for r in range(TS // ROWS):
            rows = slice(r * ROWS, (r + 1) * ROWS)
            for l in range(nl):
                ls = slice(l * LANES, (l + 1) * LANES)
                xb = _taps(xp_ref, head + r * ROWS, l * LANES, cw_ref, cb_ref, c0 + l * LANES)
                xb_ref[rows, ls] = xb
                xb16_ref[rows, ls] = xb.astype(jnp.bfloat16)

        for k in range(CW // hdim):
            hd = c * (CW // hdim) + k
            hs = slice(k * hdim, (k + 1) * hdim)
            r_ref[:, hs] = _dot(xb16_ref[:, hs], wa_ref[hd])
            i_ref[:, hs] = _dot(xb16_ref[:, hs], wx_ref[hd])

        log_s = LRU_C * jax.nn.log_sigmoid(lam_ref[:, c0:c0 + CW])
        for r in range(TS // ROWS):
            rows = slice(r * ROWS, (r + 1) * ROWS)
            for l in range(nl):
                ls = slice(l * LANES, (l + 1) * LANES)
                cs = slice(c0 + l * LANES, c0 + (l + 1) * LANES)
                rg = _sigmoid(r_ref[rows, ls] + ba_ref[:, cs])
                ig = _sigmoid(i_ref[rows, ls] + bx_ref[:, cs])
                log_a = rg * log_s[:, ls]
                a = jnp.exp(log_a)
                mult = jnp.sqrt(-jnp.tanh(log_a) * (a * a + 1.0))
                a_ref[rows, ls] = a
                b_ref[rows, ls] = mult * (ig * xb_ref[rows, ls])

        h_run = jnp.zeros((SUBLANES, CW), jnp.float32)
        p_run = jnp.ones((SUBLANES, CW), jnp.float32)
        for t in range(SEG):
            rows = slice(t * SUBLANES, (t + 1) * SUBLANES)
            a = a_ref[rows, :]
            h_run = a * h_run + b_ref[rows, :]
            p_run = a * p_run
            b_ref[rows, :] = h_run
            a_ref[rows, :] = p_run

        inflow = jnp.where(sub == 0, state_ref[:, c0:c0 + CW], 0.0)
        for j in range(1, SUBLANES):
            nxt = pltpu.roll(h_run + p_run * inflow, 1, axis=0)
            inflow = jnp.where(sub == j, nxt, inflow)
        last = (h_run + p_run * inflow)[SUBLANES - 1:SUBLANES, :]
        state_ref[:, c0:c0 + CW] = jnp.broadcast_to(last, (SUBLANES, CW))

        inflow_rows = jnp.concatenate([inflow] * (ROWS // SUBLANES), axis=0)
        for r in range(TS // ROWS):
            rows = slice(r * ROWS, (r + 1) * ROWS)
            hs_full = b_ref[rows, :] + a_ref[rows, :] * inflow_rows
            y_ref[rows, :] = (hs_full * gate_ref[rows, :]).astype(jnp.bfloat16)

        o_ref[...] += _dot(y_ref[...], wout_ref[c0:c0 + CW, :])

    @pl.loop(0, _single_trip())
    def _(_):
        for c in range(nch):
            project(c)
            recur(c)


def _mixer_b_call(x, layer, g, w_in, conv_w, conv_b, w_a, b_a, w_x, b_x, lam, w_out):
    bsz, s, d = x.shape
    width = w_out.shape[1]
    heads = w_a.shape[1]
    head = SUBLANES * (conv_w.shape[1] - 1)
    vec = lambda v: v.reshape(v.shape[0], 1, width)
    g, conv_b, b_a, b_x, lam = g[:, None, :], vec(conv_b), vec(b_a), vec(b_x), vec(lam)
    body = functools.partial(_mixer_b_body, width=width, heads=heads)
    f32 = jnp.float32
    return pl.pallas_call(
        body,
        grid=(bsz, s // TS),
        in_specs=[_row_spec(d), _layer_spec(g, 2 * layer + 1), _layer_spec(w_in, layer),
                  _layer_spec(conv_w, layer), _layer_spec(conv_b, layer),
                  _layer_spec(w_a, layer), _layer_spec(b_a, layer),
                  _layer_spec(w_x, layer), _layer_spec(b_x, layer),
                  _layer_spec(lam, layer), _layer_spec(w_out, layer)],
        out_specs=_row_spec(d),
        out_shape=jax.ShapeDtypeStruct(x.shape, x.dtype),
        scratch_shapes=[pltpu.VMEM((head, width), f32),
                        pltpu.VMEM((SUBLANES, width), f32),
                        pltpu.VMEM((TS, d), jnp.bfloat16),
                        pltpu.VMEM((head + TS, CW), f32),
                        pltpu.VMEM((head + TS, CW), f32),
                        pltpu.VMEM((TS, CW), f32),
                        pltpu.VMEM((TS, CW), f32),
                        pltpu.VMEM((TS, CW), f32),
                        pltpu.VMEM((TS, CW), jnp.bfloat16),
                        pltpu.VMEM((TS, CW), f32),
                        pltpu.VMEM((TS, CW), f32),
                        pltpu.VMEM((TS, CW), f32),
                        pltpu.VMEM((TS, CW), f32),
                        pltpu.VMEM((TS, CW), jnp.bfloat16)],
        compiler_params=_params(),
        name="mixer_rglru",
    )(x, g, w_in, conv_w, conv_b, w_a, b_a, w_x, b_x, lam, w_out)


def kernel(x, norm_mix, norm_ffn, norm_final, a_w_in, a_v_gain, a_w_s, a_b_s, a_w_out,
           b_w_in, b_conv_w, b_conv_b, b_w_a, b_b_a, b_w_x, b_b_x, b_lambda, b_w_out,
           f_w_up, f_conv_w, f_conv_b, f_w_down):
    bf = lambda w: w.astype(jnp.bfloat16)
    a_w_in, a_w_out = bf(a_w_in), bf(a_w_out)
    b_w_in, b_w_a, b_w_x, b_w_out = bf(b_w_in), bf(b_w_a), bf(b_w_x), bf(b_w_out)
    f_w_up, f_w_down = bf(f_w_up), bf(f_w_down)
    depth = norm_mix.shape[0]
    for i in range(depth):
        j = i // 2
        if i % 2 == 0:
            x = _mixer_a_call(x, j, norm_mix, a_w_in, a_v_gain, a_w_s, a_b_s, a_w_out,
                              natural_in=(i == 0))
        else:
            x = _mixer_b_call(x, j, norm_mix, b_w_in, b_conv_w, b_conv_b, b_w_a, b_b_a,
                              b_w_x, b_b_x, b_lambda, b_w_out)
        x = _ffn_call(x, i, norm_ffn, f_w_up, f_conv_w, f_conv_b, f_w_down, norm_final,
                      final_norm=(i == depth - 1), natural_out=(i == depth - 1))
    return x
```

```python
import functools

import jax
import jax.numpy as jnp
from jax import lax
from jax.experimental import pallas as pl
from jax.experimental.pallas import tpu as pltpu

EPS = 1e-6
LRU_C = 8.0
SUBLANES = 8
LANES = 128
SEG = 128
TS = SUBLANES * SEG
ROWS = 32
CW = 512
VMEM_LIMIT_BYTES = 58 * 1024 * 1024


def _rms(x, g):
    ms = jnp.mean(x * x, axis=-1, keepdims=True)
    return x * lax.rsqrt(ms + EPS) * g


def _dot(a, b):
    return jnp.dot(a, b, preferred_element_type=jnp.float32)


def _sigmoid(z):
    return 0.5 * jnp.tanh(0.5 * z) + 0.5


def _layer_spec(stacked, layer):
    n = stacked.ndim - 1
    return pl.BlockSpec((None,) + stacked.shape[1:], lambda b, s: (layer,) + (0,) * n,
                        pipeline_mode=pl.Buffered(1))


def _row_spec(d):
    return pl.BlockSpec((None, TS, d), lambda b, s: (b, s, 0))


def _params():
    return pltpu.CompilerParams(
        dimension_semantics=("arbitrary", "arbitrary"),
        vmem_limit_bytes=VMEM_LIMIT_BYTES)


def _single_trip():
    return jnp.minimum(pl.program_id(1), 0) + 1


def _fill_head(u_ref, lanes, tail_ref, tail_cols, ktaps):
    head = SUBLANES * (ktaps - 1)
    width = lanes.stop - lanes.start
    first = lax.broadcasted_iota(jnp.int32, (SUBLANES, width), 0) == 0
    for d in range(1, ktaps):
        cur = u_ref[head + TS - SUBLANES * d:head + TS - SUBLANES * (d - 1), lanes]
        prev = tail_ref[SUBLANES * (d - 1):SUBLANES * d, tail_cols]
        u_ref[head - SUBLANES * d:head - SUBLANES * (d - 1), lanes] = jnp.where(
            first, pltpu.roll(prev, 1, axis=0), pltpu.roll(cur, 1, axis=0))
        tail_ref[SUBLANES * (d - 1):SUBLANES * d, tail_cols] = cur


def _taps(u_ref, row0, lane0, cw_ref, cb_ref, c0):
    ktaps = cw_ref.shape[0]
    y = cb_ref[:, c0:c0 + LANES]
    for k in range(ktaps):
        r0 = row0 - SUBLANES * (ktaps - 1 - k)
        y = y + u_ref[r0:r0 + ROWS, lane0:lane0 + LANES] * cw_ref[k:k + 1, c0:c0 + LANES]
    return y


def _ffn_body(x_ref, g_ref, wup_ref, cw_ref, cb_ref, wdn_ref, gf_ref, o_ref,
              tail_ref, h_ref, u0_ref, u1_ref, act0_ref, act1_ref, perm_ref,
              *, d_ff, final_norm, natural_out):
    @pl.when(pl.program_id(1) == 0)
    def _():
        tail_ref[...] = jnp.zeros_like(tail_ref)

    h_ref[...] = _rms(x_ref[...], g_ref[...]).astype(jnp.bfloat16)
    o_ref[...] = x_ref[...]
    nch = d_ff // CW
    ktaps = cw_ref.shape[0]
    head = SUBLANES * (ktaps - 1)
    u_refs = (u0_ref, u1_ref)
    act_refs = (act0_ref, act1_ref)

    def up(c):
        u_ref = u_refs[c % 2]
        for half, c0 in enumerate((c * CW, d_ff + c * CW)):
            lanes = slice(half * CW, (half + 1) * CW)
            u_ref[head:, lanes] = _dot(h_ref[...], wup_ref[:, c0:c0 + CW])
            _fill_head(u_ref, lanes, tail_ref, slice(c0, c0 + CW), ktaps)

    def gate_mul(c):
        u_ref, act_ref = u_refs[c % 2], act_refs[c % 2]
        for lc in range(CW // LANES):
            for r in range(TS // ROWS):
                row0 = head + r * ROWS
                gate = _taps(u_ref, row0, lc * LANES, cw_ref, cb_ref, c * CW + lc * LANES)
                val = _taps(u_ref, row0, CW + lc * LANES, cw_ref, cb_ref,
                            d_ff + c * CW + lc * LANES)
                act = jax.nn.gelu(gate) * val
                act_ref[r * ROWS:(r + 1) * ROWS, lc * LANES:(lc + 1) * LANES] = (
                    act.astype(jnp.bfloat16))

    def down(c):
        o_ref[...] += _dot(act_refs[c % 2][...], wdn_ref[c * CW:(c + 1) * CW, :])

    @pl.loop(0, _single_trip())
    def _(_):
        up(0)
        for c in range(nch):
            if c + 1 < nch:
                up(c + 1)
            gate_mul(c)
            down(c)

    res = _rms(o_ref[...], gf_ref[...]) if final_norm else o_ref[...]
    if natural_out:
        for l in range(perm_ref.shape[0]):
            perm_ref[l] = res[:, l * LANES:(l + 1) * LANES]
        for j in range(SUBLANES):
            for l in range(perm_ref.shape[0]):
                o_ref[j * SEG:(j + 1) * SEG, l * LANES:(l + 1) * LANES] = (
                    perm_ref[l, pl.ds(j, SEG, stride=SUBLANES), :])
    elif final_norm:
        o_ref[...] = res


def _perm_scratch(d, used):
    return pltpu.VMEM((d // LANES, TS if used else SUBLANES, LANES), jnp.float32)


def _ffn_call(x, layer, g, w_up, conv_w, conv_b, w_down, g_final, *, final_norm, natural_out):
    bsz, s, d = x.shape
    d_ff = w_down.shape[1]
    head = SUBLANES * (conv_w.shape[1] - 1)
    g, conv_b = g[:, None, :], conv_b[:, None, :]
    g_final = g_final.reshape(1, 1, d)
    body = functools.partial(_ffn_body, d_ff=d_ff, final_norm=final_norm,
                             natural_out=natural_out)
    return pl.pallas_call(
        body,
        grid=(bsz, s // TS),
        in_specs=[_row_spec(d), _layer_spec(g, layer), _layer_spec(w_up, layer),
                  _layer_spec(conv_w, layer), _layer_spec(conv_b, layer),
                  _layer_spec(w_down, layer), _layer_spec(g_final, 0)],
        out_specs=_row_spec(d),
        out_shape=jax.ShapeDtypeStruct(x.shape, x.dtype),
        scratch_shapes=[pltpu.VMEM((head, 2 * d_ff), jnp.float32),
                        pltpu.VMEM((TS, d), jnp.bfloat16),
                        pltpu.VMEM((head + TS, 2 * CW), jnp.float32),
                        pltpu.VMEM((head + TS, 2 * CW), jnp.float32),
                        pltpu.VMEM((TS, CW), jnp.bfloat16),
                        pltpu.VMEM((TS, CW), jnp.bfloat16),
                        _perm_scratch(d, natural_out)],
        compiler_params=_params(),
        name="conv_ffn",
    )(x, g, w_up, conv_w, conv_b, w_down, g_final)


def _mixer_a_body(x_ref, g_ref, win_ref, vg_ref, ws_ref, bs_ref, wout_ref, o_ref,
                  h_ref, v_ref, sc_ref, u0_ref, u1_ref, sv_ref, y_ref, perm_ref,
                  *, width, groups, natural_in):
    gdim = width // groups
    nl = gdim // LANES
    u_refs = (u0_ref, u1_ref)
    if natural_in:
        for l in range(perm_ref.shape[0]):
            for j in range(SUBLANES):
                perm_ref[l, pl.ds(j, SEG, stride=SUBLANES), :] = (
                    x_ref[j * SEG:(j + 1) * SEG, l * LANES:(l + 1) * LANES])
        x = jnp.concatenate([perm_ref[l] for l in range(perm_ref.shape[0])], axis=1)
    else:
        x = x_ref[...]
    h_ref[...] = _rms(x, g_ref[...]).astype(jnp.bfloat16)
    o_ref[...] = x

    ss = jnp.zeros((TS, 1), jnp.float32)
    for c in range(groups):
        v = jax.nn.gelu(_dot(h_ref[...], win_ref[:, width + c * gdim:width + (c + 1) * gdim]))
        ss = ss + jnp.sum(v * v, axis=-1, keepdims=True)
        for l in range(nl):
            v_ref[c * nl + l] = v[:, l * LANES:(l + 1) * LANES]
    sc_ref[...] = jnp.broadcast_to(lax.rsqrt(ss * (1.0 / width) + EPS), (TS, LANES))

    row = lax.broadcasted_iota(jnp.int32, (SEG, SEG), 0)
    col = lax.broadcasted_iota(jnp.int32, (SEG, SEG), 1)
    def project_u(c):
        u_refs[c % 2][...] = jax.nn.gelu(_dot(h_ref[...], win_ref[:, c * gdim:(c + 1) * gdim]))

    project_u(0)
    for c in range(groups):
        cs = slice(c * gdim, (c + 1) * gdim)
        u_ref = u_refs[c % 2]
        if c + 1 < groups:
            project_u(c + 1)
        wt = jnp.where(row >= col, ws_ref[c], 0.0).astype(jnp.bfloat16)
        bias = bs_ref[c]
        for j in range(SUBLANES):
            rows = pl.ds(j, SEG, stride=SUBLANES)
            scale = sc_ref[rows, :]
            vn = jnp.concatenate(
                [v_ref[c * nl + l, rows, :] * scale * vg_ref[:, c * gdim + l * LANES:
                                                             c * gdim + (l + 1) * LANES]
                 for l in range(nl)], axis=1).astype(jnp.bfloat16)
            sv = _dot(wt, vn) + bias
            for l in range(nl):
                sv_ref[l, rows, :] = sv[:, l * LANES:(l + 1) * LANES]
        for l in range(nl):
            y_ref[:, l * LANES:(l + 1) * LANES] = (
                u_ref[:, l * LANES:(l + 1) * LANES] * sv_ref[l]).astype(jnp.bfloat16)
        o_ref[...] += _dot(y_ref[...], wout_ref[cs, :])


def _mixer_a_call(x, layer, g, w_in, v_gain, w_s, b_s, w_out, *, natural_in):
    bsz, s, d = x.shape
    width = w_out.shape[1]
    groups = w_s.shape[1]
    gdim = width // groups
    assert w_s.shape[2] == SEG
    g, v_gain, b_s = g[:, None, :], v_gain[:, None, :], b_s[:, :, :, None]
    body = functools.partial(_mixer_a_body, width=width, groups=groups, natural_in=natural_in)
    return pl.pallas_call(
        body,
        grid=(bsz, s // TS),
        in_specs=[_row_spec(d), _layer_spec(g, 2 * layer), _layer_spec(w_in, layer),
                  _layer_spec(v_gain, layer), _layer_spec(w_s, layer),
                  _layer_spec(b_s, layer), _layer_spec(w_out, layer)],
        out_specs=_row_spec(d),
        out_shape=jax.ShapeDtypeStruct(x.shape, x.dtype),
        scratch_shapes=[pltpu.VMEM((TS, d), jnp.bfloat16),
                        pltpu.VMEM((width // LANES, TS, LANES), jnp.float32),
                        pltpu.VMEM((TS, LANES), jnp.float32),
                        pltpu.VMEM((TS, gdim), jnp.float32),
                        pltpu.VMEM((TS, gdim), jnp.float32),
                        pltpu.VMEM((gdim // LANES, TS, LANES), jnp.float32),
                        pltpu.VMEM((TS, gdim), jnp.bfloat16),
                        _perm_scratch(d, natural_in)],
        compiler_params=_params(),
        name="mixer_gmlp",
    )(x, g, w_in, v_gain, w_s, b_s, w_out)


def _mixer_b_body(x_ref, g_ref, win_ref, cw_ref, cb_ref, wa_ref, ba_ref, wx_ref, bx_ref,
                  lam_ref, wout_ref, o_ref, tail_ref, state_ref, h_ref, xp0_ref, xp1_ref,
                  gate0_ref, gate1_ref, xb_ref, xb16_ref, r_ref, i_ref, a_ref, b_ref, y_ref,
                  *, width, heads):
    @pl.when(pl.program_id(1) == 0)
    def _():
        tail_ref[...] = jnp.zeros_like(tail_ref)
        state_ref[...] = jnp.zeros_like(state_ref)

    hdim = width // heads
    ktaps = cw_ref.shape[0]
    head = SUBLANES * (ktaps - 1)
    nch = width // CW
    nl = CW // LANES
    xp_refs = (xp0_ref, xp1_ref)
    gate_refs = (gate0_ref, gate1_ref)
    h_ref[...] = _rms(x_ref[...], g_ref[...]).astype(jnp.bfloat16)
    o_ref[...] = x_ref[...]
    sub = lax.broadcasted_iota(jnp.int32, (SUBLANES, CW), 0)

    def project(c):
        c0 = c * CW
        xp_ref = xp_refs[c % 2]
        gate_refs[c % 2][...] = jax.nn.gelu(_dot(h_ref[...], win_ref[:, c0:c0 + CW]))
        xp_ref[head:, :] = _dot(h_ref[...], win_ref[:, width + c0:width + c0 + CW])
        _fill_head(xp_ref, slice(0, CW), tail_ref, slice(c0, c0 + CW), ktaps)

    def conv_gates(c):
        c0 = c * CW
        xp_ref = xp_refs[c % 2]

        for r in range(TS // ROWS):
            rows = slice(r * ROWS, (r + 1) * ROWS)
            for l in range(nl):
                ls = slice(l * LANES, (l + 1) * LANES)
                xb = _taps(xp_ref, head + r * ROWS, l * LANES, cw_ref, cb_ref, c0 + l * LANES)
                xb_ref[rows, ls] = xb
                xb16_ref[rows, ls] = xb.astype(jnp.bfloat16)

        for k in range(CW // hdim):
            hd = c * (CW // hdim) + k
            hs = slice(k * hdim, (k + 1) * hdim)
            r_ref[:, hs] = _dot(xb16_ref[:, hs], wa_ref[hd])
            i_ref[:, hs] = _dot(xb16_ref[:, hs], wx_ref[hd])

    def recur(c):
        c0 = c * CW
        gate_ref = gate_refs[c % 2]

        log_s = LRU_C * jax.nn.log_sigmoid(lam_ref[:, c0:c0 + CW])
        for r in range(TS // ROWS):
            rows = slice(r * ROWS, (r + 1) * ROWS)
            for l in range(nl):
                ls = slice(l * LANES, (l + 1) * LANES)
                cs = slice(c0 + l * LANES, c0 + (l + 1) * LANES)
                rg = _sigmoid(r_ref[rows, ls] + ba_ref[:, cs])
                ig = _sigmoid(i_ref[rows, ls] + bx_ref[:, cs])
                log_a = rg * log_s[:, ls]
                a = jnp.exp(log_a)
                mult = jnp.sqrt(-jnp.tanh(log_a) * (a * a + 1.0))
                a_ref[rows, ls] = a
                b_ref[rows, ls] = mult * (ig * xb_ref[rows, ls])

        h_run = jnp.zeros((SUBLANES, CW), jnp.float32)
        p_run = jnp.ones((SUBLANES, CW), jnp.float32)
        for t in range(SEG):
            rows = slice(t * SUBLANES, (t + 1) * SUBLANES)
            a = a_ref[rows, :]
            h_run = a * h_run + b_ref[rows, :]
            p_run = a * p_run
            b_ref[rows, :] = h_run
            a_ref[rows, :] = p_run

        inflow = jnp.where(sub == 0, state_ref[:, c0:c0 + CW], 0.0)
        for j in range(1, SUBLANES):
            nxt = pltpu.roll(h_run + p_run * inflow, 1, axis=0)
            inflow = jnp.where(sub == j, nxt, inflow)
        last = (h_run + p_run * inflow)[SUBLANES - 1:SUBLANES, :]
        state_ref[:, c0:c0 + CW] = jnp.broadcast_to(last, (SUBLANES, CW))

        inflow_rows = jnp.concatenate([inflow] * (ROWS // SUBLANES), axis=0)
        for r in range(TS // ROWS):
            rows = slice(r * ROWS, (r + 1) * ROWS)
            hs_full = b_ref[rows, :] + a_ref[rows, :] * inflow_rows
            y_ref[rows, :] = (hs_full * gate_ref[rows, :]).astype(jnp.bfloat16)

        o_ref[...] += _dot(y_ref[...], wout_ref[c0:c0 + CW, :])

    @pl.loop(0, _single_trip())
    def _(_):
        project(0)
        for c in range(nch):
            conv_gates(c)
            if c + 1 < nch:
                project(c + 1)
            recur(c)


def _mixer_b_call(x, layer, g, w_in, conv_w, conv_b, w_a, b_a, w_x, b_x, lam, w_out):
    bsz, s, d = x.shape
    width = w_out.shape[1]
    heads = w_a.shape[1]
    head = SUBLANES * (conv_w.shape[1] - 1)
    vec = lambda v: v.reshape(v.shape[0], 1, width)
    g, conv_b, b_a, b_x, lam = g[:, None, :], vec(conv_b), vec(b_a), vec(b_x), vec(lam)
    body = functools.partial(_mixer_b_body, width=width, heads=heads)
    f32 = jnp.float32
    return pl.pallas_call(
        body,
        grid=(bsz, s // TS),
        in_specs=[_row_spec(d), _layer_spec(g, 2 * layer + 1), _layer_spec(w_in, layer),
                  _layer_spec(conv_w, layer), _layer_spec(conv_b, layer),
                  _layer_spec(w_a, layer), _layer_spec(b_a, layer),
                  _layer_spec(w_x, layer), _layer_spec(b_x, layer),
                  _layer_spec(lam, layer), _layer_spec(w_out, layer)],
        out_specs=_row_spec(d),
        out_shape=jax.ShapeDtypeStruct(x.shape, x.dtype),
        scratch_shapes=[pltpu.VMEM((head, width), f32),
                        pltpu.VMEM((SUBLANES, width), f32),
                        pltpu.VMEM((TS, d), jnp.bfloat16),
                        pltpu.VMEM((head + TS, CW), f32),
                        pltpu.VMEM((head + TS, CW), f32),
                        pltpu.VMEM((TS, CW), f32),
                        pltpu.VMEM((TS, CW), f32),
                        pltpu.VMEM((TS, CW), f32),
                        pltpu.VMEM((TS, CW), jnp.bfloat16),
                        pltpu.VMEM((TS, CW), f32),
                        pltpu.VMEM((TS, CW), f32),
                        pltpu.VMEM((TS, CW), f32),
                        pltpu.VMEM((TS, CW), f32),
                        pltpu.VMEM((TS, CW), jnp.bfloat16)],
        compiler_params=_params(),
        name="mixer_rglru",
    )(x, g, w_in, conv_w, conv_b, w_a, b_a, w_x, b_x, lam, w_out)


def kernel(x, norm_mix, norm_ffn, norm_final, a_w_in, a_v_gain, a_w_s, a_b_s, a_w_out,
           b_w_in, b_conv_w, b_conv_b, b_w_a, b_b_a, b_w_x, b_b_x, b_lambda, b_w_out,
           f_w_up, f_conv_w, f_conv_b, f_w_down):
    bf = lambda w: w.astype(jnp.bfloat16)
    a_w_in, a_w_out = bf(a_w_in), bf(a_w_out)
    b_w_in, b_w_a, b_w_x, b_w_out = bf(b_w_in), bf(b_w_a), bf(b_w_x), bf(b_w_out)
    f_w_up, f_w_down = bf(f_w_up), bf(f_w_down)
    depth = norm_mix.shape[0]
    for i in range(depth):
        j = i // 2
        if i % 2 == 0:
            x = _mixer_a_call(x, j, norm_mix, a_w_in, a_v_gain, a_w_s, a_b_s, a_w_out,
                              natural_in=(i == 0))
        else:
            x = _mixer_b_call(x, j, norm_mix, b_w_in, b_conv_w, b_conv_b, b_w_a, b_b_a,
                              b_w_x, b_b_x, b_lambda, b_w_out)
        x = _ffn_call(x, i, norm_ffn, f_w_up, f_conv_w, f_conv_b, f_w_down, norm_final,
                      final_norm=(i == depth - 1), natural_out=(i == depth - 1))
    return x
```

```python
import functools

import jax
import jax.numpy as jnp
from jax import lax
from jax.experimental import pallas as pl
from jax.experimental.pallas import tpu as pltpu

EPS = 1e-6
LRU_C = 8.0
SUBLANES = 8
LANES = 128
SEG = 128
TS = SUBLANES * SEG
ROWS = 32
CW = 512
VMEM_LIMIT_BYTES = 58 * 1024 * 1024


def _rms(x, g):
    ms = jnp.mean(x * x, axis=-1, keepdims=True)
    return x * lax.rsqrt(ms + EPS) * g


def _dot(a, b):
    return jnp.dot(a, b, preferred_element_type=jnp.float32)


def _sigmoid(z):
    return 0.5 * jnp.tanh(0.5 * z) + 0.5


def _layer_spec(stacked, layer):
    n = stacked.ndim - 1
    return pl.BlockSpec((None,) + stacked.shape[1:], lambda b, s: (layer,) + (0,) * n,
                        pipeline_mode=pl.Buffered(1))


def _row_spec(d):
    return pl.BlockSpec((None, TS, d), lambda b, s: (b, s, 0))


def _params():
    return pltpu.CompilerParams(
        dimension_semantics=("arbitrary", "arbitrary"),
        vmem_limit_bytes=VMEM_LIMIT_BYTES)


def _single_trip():
    return jnp.minimum(pl.program_id(1), 0) + 1


def _fill_head(u_ref, lanes, tail_ref, tail_cols, ktaps):
    head = SUBLANES * (ktaps - 1)
    width = lanes.stop - lanes.start
    first = lax.broadcasted_iota(jnp.int32, (SUBLANES, width), 0) == 0
    for d in range(1, ktaps):
        cur = u_ref[head + TS - SUBLANES * d:head + TS - SUBLANES * (d - 1), lanes]
        prev = tail_ref[SUBLANES * (d - 1):SUBLANES * d, tail_cols]
        u_ref[head - SUBLANES * d:head - SUBLANES * (d - 1), lanes] = jnp.where(
            first, pltpu.roll(prev, 1, axis=0), pltpu.roll(cur, 1, axis=0))
        tail_ref[SUBLANES * (d - 1):SUBLANES * d, tail_cols] = cur


def _taps(u_ref, row0, lane0, cw_ref, cb_ref, c0):
    ktaps = cw_ref.shape[0]
    y = cb_ref[:, c0:c0 + LANES]
    for k in range(ktaps):
        r0 = row0 - SUBLANES * (ktaps - 1 - k)
        y = y + u_ref[r0:r0 + ROWS, lane0:lane0 + LANES] * cw_ref[k:k + 1, c0:c0 + LANES]
    return y


def _ffn_body(x_ref, g_ref, wup_ref, cw_ref, cb_ref, wdn_ref, gf_ref, o_ref,
              tail_ref, h_ref, u0_ref, u1_ref, act0_ref, act1_ref, perm_ref,
              *, d_ff, final_norm, natural_out):
    @pl.when(pl.program_id(1) == 0)
    def _():
        tail_ref[...] = jnp.zeros_like(tail_ref)

    h_ref[...] = _rms(x_ref[...], g_ref[...]).astype(jnp.bfloat16)
    o_ref[...] = x_ref[...]
    nch = d_ff // CW
    ktaps = cw_ref.shape[0]
    head = SUBLANES * (ktaps - 1)
    u_refs = (u0_ref, u1_ref)
    act_refs = (act0_ref, act1_ref)

    def up(c):
        u_ref = u_refs[c % 2]
        for half, c0 in enumerate((c * CW, d_ff + c * CW)):
            lanes = slice(half * CW, (half + 1) * CW)
            u_ref[head:, lanes] = _dot(h_ref[...], wup_ref[:, c0:c0 + CW])
            _fill_head(u_ref, lanes, tail_ref, slice(c0, c0 + CW), ktaps)

    def gate_mul(c):
        u_ref, act_ref = u_refs[c % 2], act_refs[c % 2]
        for lc in range(CW // LANES):
            for r in range(TS // ROWS):
                row0 = head + r * ROWS
                gate = _taps(u_ref, row0, lc * LANES, cw_ref, cb_ref, c * CW + lc * LANES)
                val = _taps(u_ref, row0, CW + lc * LANES, cw_ref, cb_ref,
                            d_ff + c * CW + lc * LANES)
                act = jax.nn.gelu(gate) * val
                act_ref[r * ROWS:(r + 1) * ROWS, lc * LANES:(lc + 1) * LANES] = (
                    act.astype(jnp.bfloat16))

    def down(c):
        o_ref[...] += _dot(act_refs[c % 2][...], wdn_ref[c * CW:(c + 1) * CW, :])

    @pl.loop(0, _single_trip())
    def _(_):
        up(0)
        for c in range(nch):
            if c + 1 < nch:
                up(c + 1)
            gate_mul(c)
            down(c)

    res = _rms(o_ref[...], gf_ref[...]) if final_norm else o_ref[...]
    if natural_out:
        for l in range(perm_ref.shape[0]):
            perm_ref[l] = res[:, l * LANES:(l + 1) * LANES]
        for j in range(SUBLANES):
            for l in range(perm_ref.shape[0]):
                o_ref[j * SEG:(j + 1) * SEG, l * LANES:(l + 1) * LANES] = (
                    perm_ref[l, pl.ds(j, SEG, stride=SUBLANES), :])
    elif final_norm:
        o_ref[...] = res


def _perm_scratch(d, used):
    return pltpu.VMEM((d // LANES, TS if used else SUBLANES, LANES), jnp.float32)


def _ffn_call(x, layer, g, w_up, conv_w, conv_b, w_down, g_final, *, final_norm, natural_out):
    bsz, s, d = x.shape
    d_ff = w_down.shape[1]
    head = SUBLANES * (conv_w.shape[1] - 1)
    g, conv_b = g[:, None, :], conv_b[:, None, :]
    g_final = g_final.reshape(1, 1, d)
    body = functools.partial(_ffn_body, d_ff=d_ff, final_norm=final_norm,
                             natural_out=natural_out)
    return pl.pallas_call(
        body,
        grid=(bsz, s // TS),
        in_specs=[_row_spec(d), _layer_spec(g, layer), _layer_spec(w_up, layer),
                  _layer_spec(conv_w, layer), _layer_spec(conv_b, layer),
                  _layer_spec(w_down, layer), _layer_spec(g_final, 0)],
        out_specs=_row_spec(d),
        out_shape=jax.ShapeDtypeStruct(x.shape, x.dtype),
        scratch_shapes=[pltpu.VMEM((head, 2 * d_ff), jnp.float32),
                        pltpu.VMEM((TS, d), jnp.bfloat16),
                        pltpu.VMEM((head + TS, 2 * CW), jnp.float32),
                        pltpu.VMEM((head + TS, 2 * CW), jnp.float32),
                        pltpu.VMEM((TS, CW), jnp.bfloat16),
                        pltpu.VMEM((TS, CW), jnp.bfloat16),
                        _perm_scratch(d, natural_out)],
        compiler_params=_params(),
        name="conv_ffn",
    )(x, g, w_up, conv_w, conv_b, w_down, g_final)


def _mixer_a_body(x_ref, g_ref, win_ref, vg_ref, ws_ref, bs_ref, wout_ref, o_ref,
                  h_ref, v_ref, sc_ref, u0_ref, u1_ref, sv_ref, y_ref, perm_ref,
                  *, width, groups, natural_in):
    gdim = width // groups
    nl = gdim // LANES
    u_refs = (u0_ref, u1_ref)
    if natural_in:
        for l in range(perm_ref.shape[0]):
            for j in range(SUBLANES):
                perm_ref[l, pl.ds(j, SEG, stride=SUBLANES), :] = (
                    x_ref[j * SEG:(j + 1) * SEG, l * LANES:(l + 1) * LANES])
        x = jnp.concatenate([perm_ref[l] for l in range(perm_ref.shape[0])], axis=1)
    else:
        x = x_ref[...]
    h_ref[...] = _rms(x, g_ref[...]).astype(jnp.bfloat16)
    o_ref[...] = x

    def project(n):
        col0 = width + n * gdim if n < groups else (n - groups) * gdim
        u_refs[n % 2][...] = _dot(h_ref[...], win_ref[:, col0:col0 + gdim])

    project(0)
    for c in range(groups):
        project(c + 1)
        z_ref = u_refs[c % 2]
        for r in range(TS // ROWS):
            rows = slice(r * ROWS, (r + 1) * ROWS)
            sq = None
            for l in range(nl):
                v = jax.nn.gelu(z_ref[rows, l * LANES:(l + 1) * LANES])
                v_ref[c * nl + l, rows, :] = v
                sq = v * v if sq is None else sq + v * v
            sc_ref[rows, :] = sq if c == 0 else sc_ref[rows, :] + sq
    ss = jnp.sum(sc_ref[...], axis=-1, keepdims=True)
    sc_ref[...] = jnp.broadcast_to(lax.rsqrt(ss * (1.0 / width) + EPS), (TS, LANES))

    row = lax.broadcasted_iota(jnp.int32, (SEG, SEG), 0)
    col = lax.broadcasted_iota(jnp.int32, (SEG, SEG), 1)
    for c in range(groups):
        cs = slice(c * gdim, (c + 1) * gdim)
        u_ref = u_refs[(groups + c) % 2]
        if c + 1 < groups:
            project(groups + c + 1)
        wt = jnp.where(row >= col, ws_ref[c], 0.0).astype(jnp.bfloat16)
        bias = bs_ref[c]
        for j in range(SUBLANES):
            rows = pl.ds(j, SEG, stride=SUBLANES)
            scale = sc_ref[rows, :]
            vn = jnp.concatenate(
                [v_ref[c * nl + l, rows, :] * scale * vg_ref[:, c * gdim + l * LANES:
                                                             c * gdim + (l + 1) * LANES]
                 for l in range(nl)], axis=1).astype(jnp.bfloat16)
            sv = _dot(wt, vn) + bias
            for l in range(nl):
                sv_ref[l, rows, :] = sv[:, l * LANES:(l + 1) * LANES]
        for r in range(TS // ROWS):
            rows = slice(r * ROWS, (r + 1) * ROWS)
            for l in range(nl):
                ls = slice(l * LANES, (l + 1) * LANES)
                y_ref[rows, ls] = (jax.nn.gelu(u_ref[rows, ls]) * sv_ref[l, rows, :]).astype(
                    jnp.bfloat16)
        o_ref[...] += _dot(y_ref[...], wout_ref[cs, :])


def _mixer_a_call(x, layer, g, w_in, v_gain, w_s, b_s, w_out, *, natural_in):
    bsz, s, d = x.shape
    width = w_out.shape[1]
    groups = w_s.shape[1]
    gdim = width // groups
    assert w_s.shape[2] == SEG
    g, v_gain, b_s = g[:, None, :], v_gain[:, None, :], b_s[:, :, :, None]
    body = functools.partial(_mixer_a_body, width=width, groups=groups, natural_in=natural_in)
    return pl.pallas_call(
        body,
        grid=(bsz, s // TS),
        in_specs=[_row_spec(d), _layer_spec(g, 2 * layer), _layer_spec(w_in, layer),
                  _layer_spec(v_gain, layer), _layer_spec(w_s, layer),
                  _layer_spec(b_s, layer), _layer_spec(w_out, layer)],
        out_specs=_row_spec(d),
        out_shape=jax.ShapeDtypeStruct(x.shape, x.dtype),
        scratch_shapes=[pltpu.VMEM((TS, d), jnp.bfloat16),
                        pltpu.VMEM((width // LANES, TS, LANES), jnp.float32),
                        pltpu.VMEM((TS, LANES), jnp.float32),
                        pltpu.VMEM((TS, gdim), jnp.float32),
                        pltpu.VMEM((TS, gdim), jnp.float32),
                        pltpu.VMEM((gdim // LANES, TS, LANES), jnp.float32),
                        pltpu.VMEM((TS, gdim), jnp.bfloat16),
                        _perm_scratch(d, natural_in)],
        compiler_params=_params(),
        name="mixer_gmlp",
    )(x, g, w_in, v_gain, w_s, b_s, w_out)


def _mixer_b_body(x_ref, g_ref, win_ref, cw_ref, cb_ref, wa_ref, ba_ref, wx_ref, bx_ref,
                  lam_ref, wout_ref, o_ref, tail_ref, state_ref, h_ref, xp0_ref, xp1_ref,
                  gate0_ref, gate1_ref, xb_ref, xb16_ref, r_ref, i_ref, a_ref, b_ref, y_ref,
                  *, width, heads):
    @pl.when(pl.program_id(1) == 0)
    def _():
        tail_ref[...] = jnp.zeros_like(tail_ref)
        state_ref[...] = jnp.zeros_like(state_ref)

    hdim = width // heads
    ktaps = cw_ref.shape[0]
    head = SUBLANES * (ktaps - 1)
    nch = width // CW
    nl = CW // LANES
    xp_refs = (xp0_ref, xp1_ref)
    gate_refs = (gate0_ref, gate1_ref)
    h_ref[...] = _rms(x_ref[...], g_ref[...]).astype(jnp.bfloat16)
    o_ref[...] = x_ref[...]
    sub = lax.broadcasted_iota(jnp.int32, (SUBLANES, CW), 0)

    def project_pieces(c):
        c0 = c * CW
        xp_ref, gate_ref = xp_refs[c % 2], gate_refs[c % 2]
        pieces = []
        for k in range(CW // hdim):
            hs = slice(k * hdim, (k + 1) * hdim)

            def gate_piece(hs=hs, k=k):
                gate_ref[:, hs] = jax.nn.gelu(
                    _dot(h_ref[...], win_ref[:, c0 + k * hdim:c0 + (k + 1) * hdim]))

            def xp_piece(hs=hs, k=k):
                xp_ref[head:, hs] = _dot(
                    h_ref[...], win_ref[:, width + c0 + k * hdim:width + c0 + (k + 1) * hdim])
                if k == CW // hdim - 1:
                    _fill_head(xp_ref, slice(0, CW), tail_ref, slice(c0, c0 + CW), ktaps)

            pieces += [gate_piece, xp_piece]
        return pieces

    def conv_gates(c):
        c0 = c * CW
        xp_ref = xp_refs[c % 2]

        for r in range(TS // ROWS):
            rows = slice(r * ROWS, (r + 1) * ROWS)
            for l in range(nl):
                ls = slice(l * LANES, (l + 1) * LANES)
                xb = _taps(xp_ref, head + r * ROWS, l * LANES, cw_ref, cb_ref, c0 + l * LANES)
                xb_ref[rows, ls] = xb
                xb16_ref[rows, ls] = xb.astype(jnp.bfloat16)

        for k in range(CW // hdim):
            hd = c * (CW // hdim) + k
            hs = slice(k * hdim, (k + 1) * hdim)
            r_ref[:, hs] = _dot(xb16_ref[:, hs], wa_ref[hd])
            i_ref[:, hs] = _dot(xb16_ref[:, hs], wx_ref[hd])

    def recur(c, ahead):
        c0 = c * CW
        gate_ref = gate_refs[c % 2]
        nblk = TS // ROWS
        every = nblk // len(ahead) if ahead else nblk + 1

        log_s = LRU_C * jax.nn.log_sigmoid(lam_ref[:, c0:c0 + CW])
        for r in range(nblk):
            if ahead and r % every == 0:
                ahead[r // every]()
            rows = slice(r * ROWS, (r + 1) * ROWS)
            for l in range(nl):
                ls = slice(l * LANES, (l + 1) * LANES)
                cs = slice(c0 + l * LANES, c0 + (l + 1) * LANES)
                rg = _sigmoid(r_ref[rows, ls] + ba_ref[:, cs])
                ig = _sigmoid(i_ref[rows, ls] + bx_ref[:, cs])
                log_a = rg * log_s[:, ls]
                a = jnp.exp(log_a)
                mult = jnp.sqrt(-jnp.tanh(log_a) * (a * a + 1.0))
                a_ref[rows, ls] = a
                b_ref[rows, ls] = mult * (ig * xb_ref[rows, ls])

        h_run = jnp.zeros((SUBLANES, CW), jnp.float32)
        p_run = jnp.ones((SUBLANES, CW), jnp.float32)
        for t in range(SEG):
            rows = slice(t * SUBLANES, (t + 1) * SUBLANES)
            a = a_ref[rows, :]
            h_run = a * h_run + b_ref[rows, :]
            p_run = a * p_run
            b_ref[rows, :] = h_run
            a_ref[rows, :] = p_run

        inflow = jnp.where(sub == 0, state_ref[:, c0:c0 + CW], 0.0)
        for j in range(1, SUBLANES):
            nxt = pltpu.roll(h_run + p_run * inflow, 1, axis=0)
            inflow = jnp.where(sub == j, nxt, inflow)
        last = (h_run + p_run * inflow)[SUBLANES - 1:SUBLANES, :]
        state_ref[:, c0:c0 + CW] = jnp.broadcast_to(last, (SUBLANES, CW))

        inflow_rows = jnp.concatenate([inflow] * (ROWS // SUBLANES), axis=0)
        for r in range(TS // ROWS):
            rows = slice(r * ROWS, (r + 1) * ROWS)
            hs_full = b_ref[rows, :] + a_ref[rows, :] * inflow_rows
            y_ref[rows, :] = (hs_full * gate_ref[rows, :]).astype(jnp.bfloat16)

        o_ref[...] += _dot(y_ref[...], wout_ref[c0:c0 + CW, :])

    @pl.loop(0, _single_trip())
    def _(_):
        for piece in project_pieces(0):
            piece()
        for c in range(nch):
            conv_gates(c)
            recur(c, project_pieces(c + 1) if c + 1 < nch else [])


def _mixer_b_call(x, layer, g, w_in, conv_w, conv_b, w_a, b_a, w_x, b_x, lam, w_out):
    bsz, s, d = x.shape
    width = w_out.shape[1]
    heads = w_a.shape[1]
    head = SUBLANES * (conv_w.shape[1] - 1)
    vec = lambda v: v.reshape(v.shape[0], 1, width)
    g, conv_b, b_a, b_x, lam = g[:, None, :], vec(conv_b), vec(b_a), vec(b_x), vec(lam)
    body = functools.partial(_mixer_b_body, width=width, heads=heads)
    f32 = jnp.float32
    return pl.pallas_call(
        body,
        grid=(bsz, s // TS),
        in_specs=[_row_spec(d), _layer_spec(g, 2 * layer + 1), _layer_spec(w_in, layer),
                  _layer_spec(conv_w, layer), _layer_spec(conv_b, layer),
                  _layer_spec(w_a, layer), _layer_spec(b_a, layer),
                  _layer_spec(w_x, layer), _layer_spec(b_x, layer),
                  _layer_spec(lam, layer), _layer_spec(w_out, layer)],
        out_specs=_row_spec(d),
        out_shape=jax.ShapeDtypeStruct(x.shape, x.dtype),
        scratch_shapes=[pltpu.VMEM((head, width), f32),
                        pltpu.VMEM((SUBLANES, width), f32),
                        pltpu.VMEM((TS, d), jnp.bfloat16),
                        pltpu.VMEM((head + TS, CW), f32),
                        pltpu.VMEM((head + TS, CW), f32),
                        pltpu.VMEM((TS, CW), f32),
                        pltpu.VMEM((TS, CW), f32),
                        pltpu.VMEM((TS, CW), f32),
                        pltpu.VMEM((TS, CW), jnp.bfloat16),
                        pltpu.VMEM((TS, CW), f32),
                        pltpu.VMEM((TS, CW), f32),
                        pltpu.VMEM((TS, CW), f32),
                        pltpu.VMEM((TS, CW), f32),
                        pltpu.VMEM((TS, CW), jnp.bfloat16)],
        compiler_params=_params(),
        name="mixer_rglru",
    )(x, g, w_in, conv_w, conv_b, w_a, b_a, w_x, b_x, lam, w_out)


def kernel(x, norm_mix, norm_ffn, norm_final, a_w_in, a_v_gain, a_w_s, a_b_s, a_w_out,
           b_w_in, b_conv_w, b_conv_b, b_w_a, b_b_a, b_w_x, b_b_x, b_lambda, b_w_out,
           f_w_up, f_conv_w, f_conv_b, f_w_down):
    bf = lambda w: w.astype(jnp.bfloat16)
    a_w_in, a_w_out = bf(a_w_in), bf(a_w_out)
    b_w_in, b_w_a, b_w_x, b_w_out = bf(b_w_in), bf(b_w_a), bf(b_w_x), bf(b_w_out)
    f_w_up, f_w_down = bf(f_w_up), bf(f_w_down)
    depth = norm_mix.shape[0]
    for i in range(depth):
        j = i // 2
        if i % 2 == 0:
            x = _mixer_a_call(x, j, norm_mix, a_w_in, a_v_gain, a_w_s, a_b_s, a_w_out,
                              natural_in=(i == 0))
        else:
            x = _mixer_b_call(x, j, norm_mix, b_w_in, b_conv_w, b_conv_b, b_w_a, b_b_a,
                              b_w_x, b_b_x, b_lambda, b_w_out)
        x = _ffn_call(x, i, norm_ffn, f_w_up, f_conv_w, f_conv_b, f_w_down, norm_final,
                      final_norm=(i == depth - 1), natural_out=(i == depth - 1))
    return x
```

```python
import functools

import jax
import jax.numpy as jnp
from jax import lax
from jax.experimental import pallas as pl
from jax.experimental.pallas import tpu as pltpu

EPS = 1e-6
LRU_C = 8.0
SUBLANES = 8
LANES = 128
SEG = 128
TS = SUBLANES * SEG
ROWS = 32
CW = 512
FFN_CW = 256
VMEM_LIMIT_BYTES = 58 * 1024 * 1024


def _rms(x, g):
    ms = jnp.mean(x * x, axis=-1, keepdims=True)
    return x * lax.rsqrt(ms + EPS) * g


def _dot(a, b):
    return jnp.dot(a, b, preferred_element_type=jnp.float32)


GELU_C = 0.7978845608028654
GELU_K = 0.044715


def _gelu(x):
    hx = 0.5 * x
    return hx + hx * jnp.tanh(x * (GELU_C + (GELU_C * GELU_K) * (x * x)))


def _sqrt_nonneg(z):
    return jnp.where(z > 0.0, z * lax.rsqrt(z), 0.0)


def _sigmoid(z):
    return 0.5 * jnp.tanh(0.5 * z) + 0.5


def _layer_spec(stacked, layer):
    n = stacked.ndim - 1
    return pl.BlockSpec((None,) + stacked.shape[1:], lambda b, s: (layer,) + (0,) * n,
                        pipeline_mode=pl.Buffered(1))


def _row_spec(d):
    return pl.BlockSpec((None, TS, d), lambda b, s: (b, s, 0))


def _params():
    return pltpu.CompilerParams(
        dimension_semantics=("arbitrary", "arbitrary"),
        vmem_limit_bytes=VMEM_LIMIT_BYTES)


def _single_trip():
    return jnp.minimum(pl.program_id(1), 0) + 1


def _fill_head(u_ref, lanes, tail_ref, tail_cols, ktaps):
    head = SUBLANES * (ktaps - 1)
    width = lanes.stop - lanes.start
    first = lax.broadcasted_iota(jnp.int32, (SUBLANES, width), 0) == 0
    for d in range(1, ktaps):
        cur = u_ref[head + TS - SUBLANES * d:head + TS - SUBLANES * (d - 1), lanes]
        prev = tail_ref[SUBLANES * (d - 1):SUBLANES * d, tail_cols]
        u_ref[head - SUBLANES * d:head - SUBLANES * (d - 1), lanes] = jnp.where(
            first, pltpu.roll(prev, 1, axis=0), pltpu.roll(cur, 1, axis=0))
        tail_ref[SUBLANES * (d - 1):SUBLANES * d, tail_cols] = cur


def _taps(u_ref, row0, lane0, cw_ref, cb_ref, c0):
    ktaps = cw_ref.shape[0]
    y = cb_ref[:, c0:c0 + LANES]
    for k in range(ktaps):
        r0 = row0 - SUBLANES * (ktaps - 1 - k)
        y = y + u_ref[r0:r0 + ROWS, lane0:lane0 + LANES] * cw_ref[k:k + 1, c0:c0 + LANES]
    return y


def _ffn_body(x_ref, g_ref, wup_ref, cw_ref, cb_ref, wdn_ref, gf_ref, o_ref,
              tail_ref, h_ref, u0_ref, u1_ref, act0_ref, act1_ref, perm_ref,
              *, d_ff, final_norm, natural_out):
    @pl.when(pl.program_id(1) == 0)
    def _():
        tail_ref[...] = jnp.zeros_like(tail_ref)

    h_ref[...] = _rms(x_ref[...], g_ref[...]).astype(jnp.bfloat16)
    o_ref[...] = x_ref[...]
    wid = FFN_CW
    nch = d_ff // wid
    ktaps = cw_ref.shape[0]
    head = SUBLANES * (ktaps - 1)
    u_refs = (u0_ref, u1_ref)
    act_refs = (act0_ref, act1_ref)

    def up(c):
        u_ref = u_refs[c % 2]
        for half, c0 in enumerate((c * wid, d_ff + c * wid)):
            lanes = slice(half * wid, (half + 1) * wid)
            u_ref[head:, lanes] = _dot(h_ref[...], wup_ref[:, c0:c0 + wid])
            _fill_head(u_ref, lanes, tail_ref, slice(c0, c0 + wid), ktaps)

    def gate_mul(c):
        u_ref, act_ref = u_refs[c % 2], act_refs[c % 2]
        for lc in range(wid // LANES):
            for r in range(TS // ROWS):
                row0 = head + r * ROWS
                gate = _taps(u_ref, row0, lc * LANES, cw_ref, cb_ref, c * wid + lc * LANES)
                val = _taps(u_ref, row0, wid + lc * LANES, cw_ref, cb_ref,
                            d_ff + c * wid + lc * LANES)
                act = _gelu(gate) * val
                act_ref[r * ROWS:(r + 1) * ROWS, lc * LANES:(lc + 1) * LANES] = (
                    act.astype(jnp.bfloat16))

    def down(c):
        o_ref[...] += _dot(act_refs[c % 2][...], wdn_ref[c * wid:(c + 1) * wid, :])

    @pl.loop(0, _single_trip())
    def _(_):
        up(0)
        for c in range(nch):
            if c + 1 < nch:
                up(c + 1)
            gate_mul(c)
            down(c)

    res = _rms(o_ref[...], gf_ref[...]) if final_norm else o_ref[...]
    if natural_out:
        for l in range(perm_ref.shape[0]):
            perm_ref[l] = res[:, l * LANES:(l + 1) * LANES]
        for j in range(SUBLANES):
            for l in range(perm_ref.shape[0]):
                o_ref[j * SEG:(j + 1) * SEG, l * LANES:(l + 1) * LANES] = (
                    perm_ref[l, pl.ds(j, SEG, stride=SUBLANES), :])
    elif final_norm:
        o_ref[...] = res


def _perm_scratch(d, used):
    return pltpu.VMEM((d // LANES, TS if used else SUBLANES, LANES), jnp.float32)


def _ffn_call(x, layer, g, w_up, conv_w, conv_b, w_down, g_final, *, final_norm, natural_out):
    bsz, s, d = x.shape
    d_ff = w_down.shape[1]
    head = SUBLANES * (conv_w.shape[1] - 1)
    g, conv_b = g[:, None, :], conv_b[:, None, :]
    g_final = g_final.reshape(1, 1, d)
    body = functools.partial(_ffn_body, d_ff=d_ff, final_norm=final_norm,
                             natural_out=natural_out)
    return pl.pallas_call(
        body,
        grid=(bsz, s // TS),
        in_specs=[_row_spec(d), _layer_spec(g, layer), _layer_spec(w_up, layer),
                  _layer_spec(conv_w, layer), _layer_spec(conv_b, layer),
                  _layer_spec(w_down, layer), _layer_spec(g_final, 0)],
        out_specs=_row_spec(d),
        out_shape=jax.ShapeDtypeStruct(x.shape, x.dtype),
        scratch_shapes=[pltpu.VMEM((head, 2 * d_ff), jnp.float32),
                        pltpu.VMEM((TS, d), jnp.bfloat16),
                        pltpu.VMEM((head + TS, 2 * FFN_CW), jnp.float32),
                        pltpu.VMEM((head + TS, 2 * FFN_CW), jnp.float32),
                        pltpu.VMEM((TS, FFN_CW), jnp.bfloat16),
                        pltpu.VMEM((TS, FFN_CW), jnp.bfloat16),
                        _perm_scratch(d, natural_out)],
        compiler_params=_params(),
        name="conv_ffn",
    )(x, g, w_up, conv_w, conv_b, w_down, g_final)


def _mixer_a_body(x_ref, g_ref, win_ref, vg_ref, ws_ref, bs_ref, wout_ref, o_ref,
                  h_ref, v_ref, sc_ref, u0_ref, u1_ref, sv_ref, y_ref, perm_ref,
                  *, width, groups, natural_in):
    gdim = width // groups
    nl = gdim // LANES
    u_refs = (u0_ref, u1_ref)
    if natural_in:
        for l in range(perm_ref.shape[0]):
            for j in range(SUBLANES):
                perm_ref[l, pl.ds(j, SEG, stride=SUBLANES), :] = (
                    x_ref[j * SEG:(j + 1) * SEG, l * LANES:(l + 1) * LANES])
        x = jnp.concatenate([perm_ref[l] for l in range(perm_ref.shape[0])], axis=1)
    else:
        x = x_ref[...]
    h_ref[...] = _rms(x, g_ref[...]).astype(jnp.bfloat16)
    o_ref[...] = x

    def project(n):
        col0 = width + n * gdim if n < groups else (n - groups) * gdim
        u_refs[n % 2][...] = _dot(h_ref[...], win_ref[:, col0:col0 + gdim])

    project(0)
    for c in range(groups):
        project(c + 1)
        z_ref = u_refs[c % 2]
        for r in range(TS // ROWS):
            rows = slice(r * ROWS, (r + 1) * ROWS)
            sq = None
            for l in range(nl):
                v = _gelu(z_ref[rows, l * LANES:(l + 1) * LANES])
                v_ref[c * nl + l, rows, :] = v
                sq = v * v if sq is None else sq + v * v
            sc_ref[rows, :] = sq if c == 0 else sc_ref[rows, :] + sq
    ss = jnp.sum(sc_ref[...], axis=-1, keepdims=True)
    sc_ref[...] = jnp.broadcast_to(lax.rsqrt(ss * (1.0 / width) + EPS), (TS, LANES))

    row = lax.broadcasted_iota(jnp.int32, (SEG, SEG), 0)
    col = lax.broadcasted_iota(jnp.int32, (SEG, SEG), 1)
    for c in range(groups):
        cs = slice(c * gdim, (c + 1) * gdim)
        u_ref = u_refs[(groups + c) % 2]
        if c + 1 < groups:
            project(groups + c + 1)
        wt = jnp.where(row >= col, ws_ref[c], 0.0).astype(jnp.bfloat16)
        bias = bs_ref[c]
        for j in range(SUBLANES):
            rows = pl.ds(j, SEG, stride=SUBLANES)
            scale = sc_ref[rows, :]
            vn = jnp.concatenate(
                [v_ref[c * nl + l, rows, :] * scale * vg_ref[:, c * gdim + l * LANES:
                                                             c * gdim + (l + 1) * LANES]
                 for l in range(nl)], axis=1).astype(jnp.bfloat16)
            sv = _dot(wt, vn) + bias
            for l in range(nl):
                sv_ref[l, rows, :] = sv[:, l * LANES:(l + 1) * LANES]
        for r in range(TS // ROWS):
            rows = slice(r * ROWS, (r + 1) * ROWS)
            for l in range(nl):
                ls = slice(l * LANES, (l + 1) * LANES)
                y_ref[rows, ls] = (_gelu(u_ref[rows, ls]) * sv_ref[l, rows, :]).astype(
                    jnp.bfloat16)
        o_ref[...] += _dot(y_ref[...], wout_ref[cs, :])


def _mixer_a_call(x, layer, g, w_in, v_gain, w_s, b_s, w_out, *, natural_in):
    bsz, s, d = x.shape
    width = w_out.shape[1]
    groups = w_s.shape[1]
    gdim = width // groups
    assert w_s.shape[2] == SEG
    g, v_gain, b_s = g[:, None, :], v_gain[:, None, :], b_s[:, :, :, None]
    body = functools.partial(_mixer_a_body, width=width, groups=groups, natural_in=natural_in)
    return pl.pallas_call(
        body,
        grid=(bsz, s // TS),
        in_specs=[_row_spec(d), _layer_spec(g, 2 * layer), _layer_spec(w_in, layer),
                  _layer_spec(v_gain, layer), _layer_spec(w_s, layer),
                  _layer_spec(b_s, layer), _layer_spec(w_out, layer)],
        out_specs=_row_spec(d),
        out_shape=jax.ShapeDtypeStruct(x.shape, x.dtype),
        scratch_shapes=[pltpu.VMEM((TS, d), jnp.bfloat16),
                        pltpu.VMEM((width // LANES, TS, LANES), jnp.float32),
                        pltpu.VMEM((TS, LANES), jnp.float32),
                        pltpu.VMEM((TS, gdim), jnp.float32),
                        pltpu.VMEM((TS, gdim), jnp.float32),
                        pltpu.VMEM((gdim // LANES, TS, LANES), jnp.float32),
                        pltpu.VMEM((TS, gdim), jnp.bfloat16),
                        _perm_scratch(d, natural_in)],
        compiler_params=_params(),
        name="mixer_gmlp",
    )(x, g, w_in, v_gain, w_s, b_s, w_out)


def _mixer_b_body(x_ref, g_ref, win_ref, cw_ref, cb_ref, wa_ref, ba_ref, wx_ref, bx_ref,
                  lam_ref, wout_ref, o_ref, tail_ref, state_ref, h_ref, xp0_ref, xp1_ref,
                  gate0_ref, gate1_ref, xb_ref, xb16_ref, r_ref, i_ref, a_ref, b_ref, y_ref,
                  *, width, heads):
    @pl.when(pl.program_id(1) == 0)
    def _():
        tail_ref[...] = jnp.zeros_like(tail_ref)
        state_ref[...] = jnp.zeros_like(state_ref)

    hdim = width // heads
    ktaps = cw_ref.shape[0]
    head = SUBLANES * (ktaps - 1)
    nch = width // CW
    nl = CW // LANES
    xp_refs = (xp0_ref, xp1_ref)
    gate_refs = (gate0_ref, gate1_ref)
    h_ref[...] = _rms(x_ref[...], g_ref[...]).astype(jnp.bfloat16)
    o_ref[...] = x_ref[...]
    sub = lax.broadcasted_iota(jnp.int32, (SUBLANES, CW), 0)

    def project_pieces(c):
        c0 = c * CW
        xp_ref, gate_ref = xp_refs[c % 2], gate_refs[c % 2]
        pieces = []
        for k in range(CW // hdim):
            hs = slice(k * hdim, (k + 1) * hdim)

            def gate_piece(hs=hs, k=k):
                gate_ref[:, hs] = _gelu(
                    _dot(h_ref[...], win_ref[:, c0 + k * hdim:c0 + (k + 1) * hdim]))

            def xp_piece(hs=hs, k=k):
                xp_ref[head:, hs] = _dot(
                    h_ref[...], win_ref[:, width + c0 + k * hdim:width + c0 + (k + 1) * hdim])
                if k == CW // hdim - 1:
                    _fill_head(xp_ref, slice(0, CW), tail_ref, slice(c0, c0 + CW), ktaps)

            pieces += [gate_piece, xp_piece]
        return pieces

    def conv_gates(c):
        c0 = c * CW
        xp_ref = xp_refs[c % 2]

        for r in range(TS // ROWS):
            rows = slice(r * ROWS, (r + 1) * ROWS)
            for l in range(nl):
                ls = slice(l * LANES, (l + 1) * LANES)
                xb = _taps(xp_ref, head + r * ROWS, l * LANES, cw_ref, cb_ref, c0 + l * LANES)
                xb_ref[rows, ls] = xb
                xb16_ref[rows, ls] = xb.astype(jnp.bfloat16)

        for k in range(CW // hdim):
            hd = c * (CW // hdim) + k
            hs = slice(k * hdim, (k + 1) * hdim)
            r_ref[:, hs] = _dot(xb16_ref[:, hs], wa_ref[hd])
            i_ref[:, hs] = _dot(xb16_ref[:, hs], wx_ref[hd])

    def recur(c, ahead):
        c0 = c * CW
        gate_ref = gate_refs[c % 2]
        nblk = TS // ROWS
        every = nblk // len(ahead) if ahead else nblk + 1

        log_s = LRU_C * jax.nn.log_sigmoid(lam_ref[:, c0:c0 + CW])
        for r in range(nblk):
            if ahead and r % every == 0:
                ahead[r // every]()
            rows = slice(r * ROWS, (r + 1) * ROWS)
            for l in range(nl):
                ls = slice(l * LANES, (l + 1) * LANES)
                cs = slice(c0 + l * LANES, c0 + (l + 1) * LANES)
                rg = _sigmoid(r_ref[rows, ls] + ba_ref[:, cs])
                ig = _sigmoid(i_ref[rows, ls] + bx_ref[:, cs])
                log_a = rg * log_s[:, ls]
                a = jnp.exp(log_a)
                mult = _sqrt_nonneg(-jnp.tanh(log_a) * (a * a + 1.0))
                a_ref[rows, ls] = a
                b_ref[rows, ls] = mult * (ig * xb_ref[rows, ls])

        h_run = jnp.zeros((SUBLANES, CW), jnp.float32)
        p_run = jnp.ones((SUBLANES, CW), jnp.float32)
        for t in range(SEG):
            rows = slice(t * SUBLANES, (t + 1) * SUBLANES)
            a = a_ref[rows, :]
            h_run = a * h_run + b_ref[rows, :]
            p_run = a * p_run
            b_ref[rows, :] = h_run
            a_ref[rows, :] = p_run

        inflow = jnp.where(sub == 0, state_ref[:, c0:c0 + CW], 0.0)
        for j in range(1, SUBLANES):
            nxt = pltpu.roll(h_run + p_run * inflow, 1, axis=0)
            inflow = jnp.where(sub == j, nxt, inflow)
        last = (h_run + p_run * inflow)[SUBLANES - 1:SUBLANES, :]
        state_ref[:, c0:c0 + CW] = jnp.broadcast_to(last, (SUBLANES, CW))

        inflow_rows = jnp.concatenate([inflow] * (ROWS // SUBLANES), axis=0)
        for r in range(TS // ROWS):
            rows = slice(r * ROWS, (r + 1) * ROWS)
            hs_full = b_ref[rows, :] + a_ref[rows, :] * inflow_rows
            y_ref[rows, :] = (hs_full * gate_ref[rows, :]).astype(jnp.bfloat16)

        o_ref[...] += _dot(y_ref[...], wout_ref[c0:c0 + CW, :])

    @pl.loop(0, _single_trip())
    def _(_):
        for piece in project_pieces(0):
            piece()
        for c in range(nch):
            conv_gates(c)
            recur(c, project_pieces(c + 1) if c + 1 < nch else [])


def _mixer_b_call(x, layer, g, w_in, conv_w, conv_b, w_a, b_a, w_x, b_x, lam, w_out):
    bsz, s, d = x.shape
    width = w_out.shape[1]
    heads = w_a.shape[1]
    head = SUBLANES * (conv_w.shape[1] - 1)
    vec = lambda v: v.reshape(v.shape[0], 1, width)
    g, conv_b, b_a, b_x, lam = g[:, None, :], vec(conv_b), vec(b_a), vec(b_x), vec(lam)
    body = functools.partial(_mixer_b_body, width=width, heads=heads)
    f32 = jnp.float32
    return pl.pallas_call(
        body,
        grid=(bsz, s // TS),
        in_specs=[_row_spec(d), _layer_spec(g, 2 * layer + 1), _layer_spec(w_in, layer),
                  _layer_spec(conv_w, layer), _layer_spec(conv_b, layer),
                  _layer_spec(w_a, layer), _layer_spec(b_a, layer),
                  _layer_spec(w_x, layer), _layer_spec(b_x, layer),
                  _layer_spec(lam, layer), _layer_spec(w_out, layer)],
        out_specs=_row_spec(d),
        out_shape=jax.ShapeDtypeStruct(x.shape, x.dtype),
        scratch_shapes=[pltpu.VMEM((head, width), f32),
                        pltpu.VMEM((SUBLANES, width), f32),
                        pltpu.VMEM((TS, d), jnp.bfloat16),
                        pltpu.VMEM((head + TS, CW), f32),
                        pltpu.VMEM((head + TS, CW), f32),
                        pltpu.VMEM((TS, CW), f32),
                        pltpu.VMEM((TS, CW), f32),
                        pltpu.VMEM((TS, CW), f32),
                        pltpu.VMEM((TS, CW), jnp.bfloat16),
                        pltpu.VMEM((TS, CW), f32),
                        pltpu.VMEM((TS, CW), f32),
                        pltpu.VMEM((TS, CW), f32),
                        pltpu.VMEM((TS, CW), f32),
                        pltpu.VMEM((TS, CW), jnp.bfloat16)],
        compiler_params=_params(),
        name="mixer_rglru",
    )(x, g, w_in, conv_w, conv_b, w_a, b_a, w_x, b_x, lam, w_out)


def kernel(x, norm_mix, norm_ffn, norm_final, a_w_in, a_v_gain, a_w_s, a_b_s, a_w_out,
           b_w_in, b_conv_w, b_conv_b, b_w_a, b_b_a, b_w_x, b_b_x, b_lambda, b_w_out,
           f_w_up, f_conv_w, f_conv_b, f_w_down):
    bf = lambda w: w.astype(jnp.bfloat16)
    a_w_in, a_w_out = bf(a_w_in), bf(a_w_out)
    b_w_in, b_w_a, b_w_x, b_w_out = bf(b_w_in), bf(b_w_a), bf(b_w_x), bf(b_w_out)
    f_w_up, f_w_down = bf(f_w_up), bf(f_w_down)
    depth = norm_mix.shape[0]
    for i in range(depth):
        j = i // 2
        if i % 2 == 0:
            x = _mixer_a_call(x, j, norm_mix, a_w_in, a_v_gain, a_w_s, a_b_s, a_w_out,
                              natural_in=(i == 0))
        else:
            x = _mixer_b_call(x, j, norm_mix, b_w_in, b_conv_w, b_conv_b, b_w_a, b_b_a,
                              b_w_x, b_b_x, b_lambda, b_w_out)
        x = _ffn_call(x, i, norm_ffn, f_w_up, f_conv_w, f_conv_b, f_w_down, norm_final,
                      final_norm=(i == depth - 1), natural_out=(i == depth - 1))
    return x
```

```python
import functools

import jax
import jax.numpy as jnp
from jax import lax
from jax.experimental import pallas as pl
from jax.experimental.pallas import tpu as pltpu

EPS = 1e-6
LRU_C = 8.0
SUBLANES = 8
LANES = 128
SEG = 128
TS = SUBLANES * SEG
ROWS = 32
CW = 512
VMEM_LIMIT_BYTES = 58 * 1024 * 1024


def _rms(x, g):
    ms = jnp.mean(x * x, axis=-1, keepdims=True)
    return x * lax.rsqrt(ms + EPS) * g


def _dot(a, b):
    return jnp.dot(a, b, preferred_element_type=jnp.float32)


GELU_C = 0.7978845608028654
GELU_K = 0.044715


def _gelu(x):
    hx = 0.5 * x
    return hx + hx * jnp.tanh(x * (GELU_C + (GELU_C * GELU_K) * (x * x)))


def _sqrt_nonneg(z):
    return jnp.where(z > 0.0, z * lax.rsqrt(z), 0.0)


def _sigmoid(z):
    return 0.5 * jnp.tanh(0.5 * z) + 0.5


def _layer_spec(stacked, layer):
    n = stacked.ndim - 1
    return pl.BlockSpec((None,) + stacked.shape[1:], lambda b, s: (layer,) + (0,) * n,
                        pipeline_mode=pl.Buffered(1))


def _row_spec(d):
    return pl.BlockSpec((None, TS, d), lambda b, s: (b, s, 0))


def _params():
    return pltpu.CompilerParams(
        dimension_semantics=("arbitrary", "arbitrary"),
        vmem_limit_bytes=VMEM_LIMIT_BYTES)


def _single_trip():
    return jnp.minimum(pl.program_id(1), 0) + 1


def _fill_head(u_ref, lanes, tail_ref, tail_cols, ktaps):
    head = SUBLANES * (ktaps - 1)
    width = lanes.stop - lanes.start
    first = lax.broadcasted_iota(jnp.int32, (SUBLANES, width), 0) == 0
    for d in range(1, ktaps):
        cur = u_ref[head + TS - SUBLANES * d:head + TS - SUBLANES * (d - 1), lanes]
        prev = tail_ref[SUBLANES * (d - 1):SUBLANES * d, tail_cols]
        u_ref[head - SUBLANES * d:head - SUBLANES * (d - 1), lanes] = jnp.where(
            first, pltpu.roll(prev, 1, axis=0), pltpu.roll(cur, 1, axis=0))
        tail_ref[SUBLANES * (d - 1):SUBLANES * d, tail_cols] = cur


def _taps(u_ref, row0, lane0, cw_ref, cb_ref, c0):
    ktaps = cw_ref.shape[0]
    y = cb_ref[:, c0:c0 + LANES]
    for k in range(ktaps):
        r0 = row0 - SUBLANES * (ktaps - 1 - k)
        y = y + u_ref[r0:r0 + ROWS, lane0:lane0 + LANES] * cw_ref[k:k + 1, c0:c0 + LANES]
    return y


def _ffn_body(x_ref, g_ref, wup_ref, cw_ref, cb_ref, wdn_ref, gf_ref, o_ref,
              tail_ref, h_ref, u0_ref, u1_ref, act0_ref, act1_ref, perm_ref,
              *, d_ff, final_norm, natural_out):
    @pl.when(pl.program_id(1) == 0)
    def _():
        tail_ref[...] = jnp.zeros_like(tail_ref)

    h_ref[...] = _rms(x_ref[...], g_ref[...]).astype(jnp.bfloat16)
    o_ref[...] = x_ref[...]
    nch = d_ff // CW
    ktaps = cw_ref.shape[0]
    head = SUBLANES * (ktaps - 1)
    u_refs = (u0_ref, u1_ref)
    act_refs = (act0_ref, act1_ref)

    def up(c):
        u_ref = u_refs[c % 2]
        for half, c0 in enumerate((c * CW, d_ff + c * CW)):
            lanes = slice(half * CW, (half + 1) * CW)
            u_ref[head:, lanes] = _dot(h_ref[...], wup_ref[:, c0:c0 + CW])
            _fill_head(u_ref, lanes, tail_ref, slice(c0, c0 + CW), ktaps)

    def gate_mul(c):
        u_ref, act_ref = u_refs[c % 2], act_refs[c % 2]
        for lc in range(CW // LANES):
            for r in range(TS // ROWS):
                row0 = head + r * ROWS
                gate = _taps(u_ref, row0, lc * LANES, cw_ref, cb_ref, c * CW + lc * LANES)
                val = _taps(u_ref, row0, CW + lc * LANES, cw_ref, cb_ref,
                            d_ff + c * CW + lc * LANES)
                act = _gelu(gate) * val
                act_ref[r * ROWS:(r + 1) * ROWS, lc * LANES:(lc + 1) * LANES] = (
                    act.astype(jnp.bfloat16))

    def down(c):
        o_ref[...] += _dot(act_refs[c % 2][...], wdn_ref[c * CW:(c + 1) * CW, :])

    @pl.loop(0, _single_trip())
    def _(_):
        up(0)
        for c in range(nch):
            if c + 1 < nch:
                up(c + 1)
            gate_mul(c)
            down(c)

    res = _rms(o_ref[...], gf_ref[...]) if final_norm else o_ref[...]
    if natural_out:
        for l in range(perm_ref.shape[0]):
            perm_ref[l] = res[:, l * LANES:(l + 1) * LANES]
        for j in range(SUBLANES):
            for l in range(perm_ref.shape[0]):
                o_ref[j * SEG:(j + 1) * SEG, l * LANES:(l + 1) * LANES] = (
                    perm_ref[l, pl.ds(j, SEG, stride=SUBLANES), :])
    elif final_norm:
        o_ref[...] = res


def _perm_scratch(d, used):
    return pltpu.VMEM((d // LANES, TS if used else SUBLANES, LANES), jnp.float32)


def _ffn_call(x, layer, g, w_up, conv_w, conv_b, w_down, g_final, *, final_norm, natural_out):
    bsz, s, d = x.shape
    d_ff = w_down.shape[1]
    head = SUBLANES * (conv_w.shape[1] - 1)
    g, conv_b = g[:, None, :], conv_b[:, None, :]
    g_final = g_final.reshape(1, 1, d)
    body = functools.partial(_ffn_body, d_ff=d_ff, final_norm=final_norm,
                             natural_out=natural_out)
    return pl.pallas_call(
        body,
        grid=(bsz, s // TS),
        in_specs=[_row_spec(d), _layer_spec(g, layer), _layer_spec(w_up, layer),
                  _layer_spec(conv_w, layer), _layer_spec(conv_b, layer),
                  _layer_spec(w_down, layer), _layer_spec(g_final, 0)],
        out_specs=_row_spec(d),
        out_shape=jax.ShapeDtypeStruct(x.shape, x.dtype),
        scratch_shapes=[pltpu.VMEM((head, 2 * d_ff), jnp.float32),
                        pltpu.VMEM((TS, d), jnp.bfloat16),
                        pltpu.VMEM((head + TS, 2 * CW), jnp.float32),
                        pltpu.VMEM((head + TS, 2 * CW), jnp.float32),
                        pltpu.VMEM((TS, CW), jnp.bfloat16),
                        pltpu.VMEM((TS, CW), jnp.bfloat16),
                        _perm_scratch(d, natural_out)],
        compiler_params=_params(),
        name="conv_ffn",
    )(x, g, w_up, conv_w, conv_b, w_down, g_final)


def _mixer_a_body(x_ref, g_ref, win_ref, vg_ref, ws_ref, bs_ref, wout_ref, o_ref,
                  h_ref, v_ref, sc_ref, u0_ref, u1_ref, sv_ref, y_ref, perm_ref,
                  *, width, groups, natural_in):
    gdim = width // groups
    nl = gdim // LANES
    u_refs = (u0_ref, u1_ref)
    if natural_in:
        for l in range(perm_ref.shape[0]):
            for j in range(SUBLANES):
                perm_ref[l, pl.ds(j, SEG, stride=SUBLANES), :] = (
                    x_ref[j * SEG:(j + 1) * SEG, l * LANES:(l + 1) * LANES])
        x = jnp.concatenate([perm_ref[l] for l in range(perm_ref.shape[0])], axis=1)
    else:
        x = x_ref[...]
    h_ref[...] = _rms(x, g_ref[...]).astype(jnp.bfloat16)
    o_ref[...] = x

    def project(n):
        col0 = width + n * gdim if n < groups else (n - groups) * gdim
        u_refs[n % 2][...] = _dot(h_ref[...], win_ref[:, col0:col0 + gdim])

    project(0)
    for c in range(groups):
        project(c + 1)
        z_ref = u_refs[c % 2]
        for r in range(TS // ROWS):
            rows = slice(r * ROWS, (r + 1) * ROWS)
            sq = None
            for l in range(nl):
                v = _gelu(z_ref[rows, l * LANES:(l + 1) * LANES])
                v_ref[c * nl + l, rows, :] = v
                sq = v * v if sq is None else sq + v * v
            sc_ref[rows, :] = sq if c == 0 else sc_ref[rows, :] + sq
    ss = jnp.sum(sc_ref[...], axis=-1, keepdims=True)
    sc_ref[...] = jnp.broadcast_to(lax.rsqrt(ss * (1.0 / width) + EPS), (TS, LANES))

    row = lax.broadcasted_iota(jnp.int32, (SEG, SEG), 0)
    col = lax.broadcasted_iota(jnp.int32, (SEG, SEG), 1)
    for c in range(groups):
        cs = slice(c * gdim, (c + 1) * gdim)
        u_ref = u_refs[(groups + c) % 2]
        if c + 1 < groups:
            project(groups + c + 1)
        wt = jnp.where(row >= col, ws_ref[c], 0.0).astype(jnp.bfloat16)
        bias = bs_ref[c]
        vns = []
        for j in range(SUBLANES):
            rows = pl.ds(j, SEG, stride=SUBLANES)
            scale = sc_ref[rows, :]
            vns += [(v_ref[c * nl + l, rows, :] * scale
                     * vg_ref[:, c * gdim + l * LANES:c * gdim + (l + 1) * LANES]
                     ).astype(jnp.bfloat16) for l in range(nl)]
        sv = _dot(wt, jnp.concatenate(vns, axis=1)) + bias
        for j in range(SUBLANES):
            rows = pl.ds(j, SEG, stride=SUBLANES)
            for l in range(nl):
                k0 = (j * nl + l) * LANES
                sv_ref[l, rows, :] = sv[:, k0:k0 + LANES]
        for r in range(TS // ROWS):
            rows = slice(r * ROWS, (r + 1) * ROWS)
            for l in range(nl):
                ls = slice(l * LANES, (l + 1) * LANES)
                y_ref[rows, ls] = (_gelu(u_ref[rows, ls]) * sv_ref[l, rows, :]).astype(
                    jnp.bfloat16)
        o_ref[...] += _dot(y_ref[...], wout_ref[cs, :])


def _mixer_a_call(x, layer, g, w_in, v_gain, w_s, b_s, w_out, *, natural_in):
    bsz, s, d = x.shape
    width = w_out.shape[1]
    groups = w_s.shape[1]
    gdim = width // groups
    assert w_s.shape[2] == SEG
    g, v_gain, b_s = g[:, None, :], v_gain[:, None, :], b_s[:, :, :, None]
    body = functools.partial(_mixer_a_body, width=width, groups=groups, natural_in=natural_in)
    return pl.pallas_call(
        body,
        grid=(bsz, s // TS),
        in_specs=[_row_spec(d), _layer_spec(g, 2 * layer), _layer_spec(w_in, layer),
                  _layer_spec(v_gain, layer), _layer_spec(w_s, layer),
                  _layer_spec(b_s, layer), _layer_spec(w_out, layer)],
        out_specs=_row_spec(d),
        out_shape=jax.ShapeDtypeStruct(x.shape, x.dtype),
        scratch_shapes=[pltpu.VMEM((TS, d), jnp.bfloat16),
                        pltpu.VMEM((width // LANES, TS, LANES), jnp.float32),
                        pltpu.VMEM((TS, LANES), jnp.float32),
                        pltpu.VMEM((TS, gdim), jnp.float32),
                        pltpu.VMEM((TS, gdim), jnp.float32),
                        pltpu.VMEM((gdim // LANES, TS, LANES), jnp.float32),
                        pltpu.VMEM((TS, gdim), jnp.bfloat16),
                        _perm_scratch(d, natural_in)],
        compiler_params=_params(),
        name="mixer_gmlp",
    )(x, g, w_in, v_gain, w_s, b_s, w_out)


def _mixer_b_body(x_ref, g_ref, win_ref, cw_ref, cb_ref, wa_ref, ba_ref, wx_ref, bx_ref,
                  lam_ref, wout_ref, o_ref, tail_ref, state_ref, h_ref, xp0_ref, xp1_ref,
                  gate0_ref, gate1_ref, xb_ref, xb16_ref, r_ref, i_ref, a_ref, b_ref, y_ref,
                  *, width, heads):
    @pl.when(pl.program_id(1) == 0)
    def _():
        tail_ref[...] = jnp.zeros_like(tail_ref)
        state_ref[...] = jnp.zeros_like(state_ref)

    hdim = width // heads
    ktaps = cw_ref.shape[0]
    head = SUBLANES * (ktaps - 1)
    nch = width // CW
    nl = CW // LANES
    xp_refs = (xp0_ref, xp1_ref)
    gate_refs = (gate0_ref, gate1_ref)
    h_ref[...] = _rms(x_ref[...], g_ref[...]).astype(jnp.bfloat16)
    o_ref[...] = x_ref[...]
    sub = lax.broadcasted_iota(jnp.int32, (SUBLANES, CW), 0)

    def project_pieces(c):
        c0 = c * CW
        xp_ref, gate_ref = xp_refs[c % 2], gate_refs[c % 2]
        pieces = []
        for k in range(CW // hdim):
            hs = slice(k * hdim, (k + 1) * hdim)

            def gate_piece(hs=hs, k=k):
                gate_ref[:, hs] = _gelu(
                    _dot(h_ref[...], win_ref[:, c0 + k * hdim:c0 + (k + 1) * hdim]))

            def xp_piece(hs=hs, k=k):
                xp_ref[head:, hs] = _dot(
                    h_ref[...], win_ref[:, width + c0 + k * hdim:width + c0 + (k + 1) * hdim])
                if k == CW // hdim - 1:
                    _fill_head(xp_ref, slice(0, CW), tail_ref, slice(c0, c0 + CW), ktaps)

            pieces += [gate_piece, xp_piece]
        return pieces

    def conv_gates(c):
        c0 = c * CW
        xp_ref = xp_refs[c % 2]

        for r in range(TS // ROWS):
            rows = slice(r * ROWS, (r + 1) * ROWS)
            for l in range(nl):
                ls = slice(l * LANES, (l + 1) * LANES)
                xb = _taps(xp_ref, head + r * ROWS, l * LANES, cw_ref, cb_ref, c0 + l * LANES)
                xb_ref[rows, ls] = xb
                xb16_ref[rows, ls] = xb.astype(jnp.bfloat16)

        for k in range(CW // hdim):
            hd = c * (CW // hdim) + k
            hs = slice(k * hdim, (k + 1) * hdim)
            r_ref[:, hs] = _dot(xb16_ref[:, hs], wa_ref[hd])
            i_ref[:, hs] = _dot(xb16_ref[:, hs], wx_ref[hd])

    def recur(c, ahead):
        c0 = c * CW
        gate_ref = gate_refs[c % 2]
        nblk = TS // ROWS
        every = nblk // len(ahead) if ahead else nblk + 1

        log_s = LRU_C * jax.nn.log_sigmoid(lam_ref[:, c0:c0 + CW])
        for r in range(nblk):
            if ahead and r % every == 0:
                ahead[r // every]()
            rows = slice(r * ROWS, (r + 1) * ROWS)
            for l in range(nl):
                ls = slice(l * LANES, (l + 1) * LANES)
                cs = slice(c0 + l * LANES, c0 + (l + 1) * LANES)
                rg = _sigmoid(r_ref[rows, ls] + ba_ref[:, cs])
                ig = _sigmoid(i_ref[rows, ls] + bx_ref[:, cs])
                log_a = rg * log_s[:, ls]
                a = jnp.exp(log_a)
                mult = _sqrt_nonneg(-jnp.tanh(log_a) * (a * a + 1.0))
                a_ref[rows, ls] = a
                b_ref[rows, ls] = mult * (ig * xb_ref[rows, ls])

        h_run = jnp.zeros((SUBLANES, CW), jnp.float32)
        p_run = jnp.ones((SUBLANES, CW), jnp.float32)
        for t in range(SEG):
            rows = slice(t * SUBLANES, (t + 1) * SUBLANES)
            a = a_ref[rows, :]
            h_run = a * h_run + b_ref[rows, :]
            p_run = a * p_run
            b_ref[rows, :] = h_run
            a_ref[rows, :] = p_run

        inflow = jnp.where(sub == 0, state_ref[:, c0:c0 + CW], 0.0)
        for j in range(1, SUBLANES):
            nxt = pltpu.roll(h_run + p_run * inflow, 1, axis=0)
            inflow = jnp.where(sub == j, nxt, inflow)
        last = (h_run + p_run * inflow)[SUBLANES - 1:SUBLANES, :]
        state_ref[:, c0:c0 + CW] = jnp.broadcast_to(last, (SUBLANES, CW))

        inflow_rows = jnp.concatenate([inflow] * (ROWS // SUBLANES), axis=0)
        for r in range(TS // ROWS):
            rows = slice(r * ROWS, (r + 1) * ROWS)
            hs_full = b_ref[rows, :] + a_ref[rows, :] * inflow_rows
            y_ref[rows, :] = (hs_full * gate_ref[rows, :]).astype(jnp.bfloat16)

        o_ref[...] += _dot(y_ref[...], wout_ref[c0:c0 + CW, :])

    @pl.loop(0, _single_trip())
    def _(_):
        for piece in project_pieces(0):
            piece()
        for c in range(nch):
            conv_gates(c)
            recur(c, project_pieces(c + 1) if c + 1 < nch else [])


def _mixer_b_call(x, layer, g, w_in, conv_w, conv_b, w_a, b_a, w_x, b_x, lam, w_out):
    bsz, s, d = x.shape
    width = w_out.shape[1]
    heads = w_a.shape[1]
    head = SUBLANES * (conv_w.shape[1] - 1)
    vec = lambda v: v.reshape(v.shape[0], 1, width)
    g, conv_b, b_a, b_x, lam = g[:, None, :], vec(conv_b), vec(b_a), vec(b_x), vec(lam)
    body = functools.partial(_mixer_b_body, width=width, heads=heads)
    f32 = jnp.float32
    return pl.pallas_call(
        body,
        grid=(bsz, s // TS),
        in_specs=[_row_spec(d), _layer_spec(g, 2 * layer + 1), _layer_spec(w_in, layer),
                  _layer_spec(conv_w, layer), _layer_spec(conv_b, layer),
                  _layer_spec(w_a, layer), _layer_spec(b_a, layer),
                  _layer_spec(w_x, layer), _layer_spec(b_x, layer),
                  _layer_spec(lam, layer), _layer_spec(w_out, layer)],
        out_specs=_row_spec(d),
        out_shape=jax.ShapeDtypeStruct(x.shape, x.dtype),
        scratch_shapes=[pltpu.VMEM((head, width), f32),
                        pltpu.VMEM((SUBLANES, width), f32),
                        pltpu.VMEM((TS, d), jnp.bfloat16),
                        pltpu.VMEM((head + TS, CW), f32),
                        pltpu.VMEM((head + TS, CW), f32),
                        pltpu.VMEM((TS, CW), f32),
                        pltpu.VMEM((TS, CW), f32),
                        pltpu.VMEM((TS, CW), f32),
                        pltpu.VMEM((TS, CW), jnp.bfloat16),
                        pltpu.VMEM((TS, CW), f32),
                        pltpu.VMEM((TS, CW), f32),
                        pltpu.VMEM((TS, CW), f32),
                        pltpu.VMEM((TS, CW), f32),
                        pltpu.VMEM((TS, CW), jnp.bfloat16)],
        compiler_params=_params(),
        name="mixer_rglru",
    )(x, g, w_in, conv_w, conv_b, w_a, b_a, w_x, b_x, lam, w_out)


def kernel(x, norm_mix, norm_ffn, norm_final, a_w_in, a_v_gain, a_w_s, a_b_s, a_w_out,
           b_w_in, b_conv_w, b_conv_b, b_w_a, b_b_a, b_w_x, b_b_x, b_lambda, b_w_out,
           f_w_up, f_conv_w, f_conv_b, f_w_down):
    bf = lambda w: w.astype(jnp.bfloat16)
    a_w_in, a_w_out = bf(a_w_in), bf(a_w_out)
    b_w_in, b_w_a, b_w_x, b_w_out = bf(b_w_in), bf(b_w_a), bf(b_w_x), bf(b_w_out)
    f_w_up, f_w_down = bf(f_w_up), bf(f_w_down)
    depth = norm_mix.shape[0]
    for i in range(depth):
        j = i // 2
        if i % 2 == 0:
            x = _mixer_a_call(x, j, norm_mix, a_w_in, a_v_gain, a_w_s, a_b_s, a_w_out,
                              natural_in=(i == 0))
        else:
            x = _mixer_b_call(x, j, norm_mix, b_w_in, b_conv_w, b_conv_b, b_w_a, b_b_a,
                              b_w_x, b_b_x, b_lambda, b_w_out)
        x = _ffn_call(x, i, norm_ffn, f_w_up, f_conv_w, f_conv_b, f_w_down, norm_final,
                      final_norm=(i == depth - 1), natural_out=(i == depth - 1))
    return x
```
